```python
import math
import jax, jax.numpy as jnp
from jax import lax
import numpy as np

D_MODEL = 2048
BATCH = 2
SEQ = 16384
DEPTH = 1

HEAD_DIM = 128
N_FOX_HEADS = 8
N_DSA_HEADS = 8
FOX_W = N_FOX_HEADS * HEAD_DIM
DSA_W = N_DSA_HEADS * HEAD_DIM
KV_LORA = 256
N_IDX_HEADS = 16
IDX_DIM = 64
TOPK_MAX = 256
N_BUCKETS = 32
MAX_DISTANCE = 128
D_FF = 5632
CONV_WIDTH = 3
Q_BLOCK = 128
EPS = 1e-6
NEG_INF = -1e30

PROJ_SIZES = (FOX_W, FOX_W, FOX_W, N_FOX_HEADS, DSA_W, KV_LORA, N_IDX_HEADS * IDX_DIM, IDX_DIM,
              N_IDX_HEADS, D_MODEL, D_MODEL)
PROJ_WIDTH = 3 * FOX_W + N_FOX_HEADS + DSA_W + KV_LORA + N_IDX_HEADS * IDX_DIM + IDX_DIM + N_IDX_HEADS + 2 * D_MODEL

kernel_name = 'hybrid_fox_dsa_convffn_adaln'


def rms_norm(x, g):
    xf = x.astype(jnp.float32)
    y = xf * lax.rsqrt(jnp.mean(xf * xf, axis=-1, keepdims=True) + EPS)
    return (y * g).astype(x.dtype)


def _split_cols(t, sizes):
    out, start = [], 0
    for s in sizes:
        out.append(t[..., start:start + s])
        start += s
    return out


def t5_bucket(n):
    n = jnp.maximum(n, 0)
    max_exact = N_BUCKETS // 2
    nf = jnp.maximum(n, 1).astype(jnp.float32)
    large = max_exact + (jnp.log(nf / max_exact) / math.log(MAX_DISTANCE / max_exact)
                         * (N_BUCKETS - max_exact)).astype(jnp.int32)
    large = jnp.minimum(large, N_BUCKETS - 1)
    return jnp.where(n < max_exact, n, large)


def fox_attention(q, k, v, logf):
    B, L, H, Dh = q.shape
    cum_t = lax.cumsum(logf, axis=1).transpose(0, 2, 1)
    kpos = jnp.arange(L)
    scale = Dh ** -0.5

    def block(i):
        s0 = i * Q_BLOCK
        qb = lax.dynamic_slice_in_dim(q, s0, Q_BLOCK, axis=1)
        cb = lax.dynamic_slice_in_dim(cum_t, s0, Q_BLOCK, axis=2)
        logits = jnp.einsum('bqhd,bkhd->bhqk', qb, k).astype(jnp.float32) * scale
        logits = logits + cb[..., None] - cum_t[:, :, None, :]
        qpos = s0 + jnp.arange(Q_BLOCK)
        mask = kpos[None, :] <= qpos[:, None]
        logits = jnp.where(mask, logits, NEG_INF)
        p = jax.nn.softmax(logits, axis=-1)
        return jnp.einsum('bhqk,bkhd->bqhd', p.astype(v.dtype), v)

    out = lax.map(block, jnp.arange(L // Q_BLOCK))
    return out.transpose(1, 0, 2, 3, 4).reshape(B, L, H, Dh)


def dsa_attention(q, k, v, q_idx, k_idx, w_idx, rel_bias, topk):
    B, L, H, Dh = q.shape
    kpos = jnp.arange(L)
    scale = Dh ** -0.5
    idx_scale = IDX_DIM ** -0.5
    gather = jax.vmap(lambda kk, ii: kk[ii])

    def block(i):
        s0 = i * Q_BLOCK
        qb = lax.dynamic_slice_in_dim(q, s0, Q_BLOCK, axis=1)
        qib = lax.dynamic_slice_in_dim(q_idx, s0, Q_BLOCK, axis=1)
        wb = lax.dynamic_slice_in_dim(w_idx, s0, Q_BLOCK, axis=1)
        qpos = s0 + jnp.arange(Q_BLOCK)
        rel = jax.nn.relu(jnp.einsum('bqjd,bkd->bqjk', qib, k_idx) * idx_scale)
        scores = jnp.einsum('bqj,bqjk->bqk', wb, rel).astype(jnp.float32)
        causal = kpos[None, :] <= qpos[:, None]
        scores = jnp.where(causal[None], scores, NEG_INF)
        _, sel = lax.top_k(scores, topk)
        valid = sel <= qpos[None, :, None]
        ks = gather(k, sel)
        vs = gather(v, sel)
        logits = jnp.einsum('bqhd,bqkhd->bhqk', qb, ks).astype(jnp.float32) * scale
        bias = rel_bias[t5_bucket(qpos[None, :, None] - sel)].astype(jnp.float32)
        logits = logits + bias.transpose(0, 3, 1, 2)
        logits = jnp.where(valid[:, None], logits, NEG_INF)
        p = jax.nn.softmax(logits, axis=-1)
        return jnp.einsum('bhqk,bqkhd->bqhd', p.astype(vs.dtype), vs)

    out = lax.map(block, jnp.arange(L // Q_BLOCK))
    return out.transpose(1, 0, 2, 3, 4).reshape(B, L, H, Dh)


def causal_dwconv(u, w, b):
    C = u.shape[-1]
    y = lax.conv_general_dilated(u, w[:, None, :], window_strides=(1,), padding=[(CONV_WIDTH - 1, 0)],
                                 dimension_numbers=('NWC', 'WIO', 'NWC'), feature_group_count=C)
    return y + b


def setup_inputs(seed: int = 0) -> dict:
    key = jax.random.key(seed)
    ks = jax.random.split(key, 24)

    def nrm(k, shape, scale):
        return jax.random.normal(k, shape, jnp.float32) * scale

    def gain(k, shape):
        return 1.0 + 0.02 * jax.random.normal(k, shape, jnp.float32)

    return {
        'x': nrm(ks[0], (BATCH, SEQ, D_MODEL), 1.0),
        'c': nrm(ks[1], (BATCH, D_MODEL), 1.0),
        'w_ada': nrm(ks[2], (DEPTH, D_MODEL, 6 * D_MODEL), D_MODEL ** -0.5),
        'b_ada': nrm(ks[3], (DEPTH, 6 * D_MODEL), 0.02),
        'norm1_g': gain(ks[4], (DEPTH, D_MODEL)),
        'w_in': nrm(ks[5], (DEPTH, D_MODEL, PROJ_WIDTH), D_MODEL ** -0.5),
        'b_forget': 3.0 + nrm(ks[6], (DEPTH, N_FOX_HEADS), 0.5),
        'q_norm_fox': gain(ks[7], (DEPTH, HEAD_DIM)),
        'k_norm_fox': gain(ks[8], (DEPTH, HEAD_DIM)),
        'kv_norm_g': gain(ks[9], (DEPTH, KV_LORA)),
        'w_ukv': nrm(ks[10], (DEPTH, KV_LORA, 2 * DSA_W), KV_LORA ** -0.5),
        'q_norm_dsa': gain(ks[11], (DEPTH, HEAD_DIM)),
        'k_norm_dsa': gain(ks[12], (DEPTH, HEAD_DIM)),
        'w_out_fox': nrm(ks[13], (DEPTH, FOX_W, D_MODEL), FOX_W ** -0.5),
        'w_out_dsa': nrm(ks[14], (DEPTH, DSA_W, D_MODEL), DSA_W ** -0.5),
        'w_out': nrm(ks[15], (DEPTH, D_MODEL, D_MODEL), D_MODEL ** -0.5),
        'norm2_g': gain(ks[16], (DEPTH, D_MODEL)),
        'w_ffn_in': nrm(ks[17], (DEPTH, D_MODEL, 2 * D_FF), D_MODEL ** -0.5),
        'conv_w': nrm(ks[18], (DEPTH, CONV_WIDTH, 2 * D_FF), CONV_WIDTH ** -0.5),
        'conv_b': nrm(ks[19], (DEPTH, 2 * D_FF), 0.02),
        'w_ffn_out': nrm(ks[20], (DEPTH, D_FF, D_MODEL), D_FF ** -0.5),
        'rel_bias': nrm(ks[21], (N_BUCKETS, N_DSA_HEADS), 0.5),
    }


def reference(x, c, w_ada, b_ada, norm1_g, w_in, b_forget, q_norm_fox, k_norm_fox, kv_norm_g, w_ukv,
              q_norm_dsa, k_norm_dsa, w_out_fox, w_out_dsa, w_out, norm2_g, w_ffn_in, conv_w, conv_b,
              w_ffn_out, rel_bias):
    B, L, _ = x.shape
    topk = min(TOPK_MAX, L // 4)
    c_act = jax.nn.silu(c)
    for l in range(DEPTH):
        mod = c_act @ w_ada[l] + b_ada[l]
        sh1, sc1, g1, sh2, sc2, g2 = [m[:, None, :] for m in jnp.split(mod, 6, axis=-1)]

        h = rms_norm(x, norm1_g[l]) * (1 + sc1) + sh1
        proj = h @ w_in[l]
        qf, kf, vf, fgate, qd, ckv, qi, ki, wi, ga, gb = _split_cols(proj, PROJ_SIZES)

        qf = rms_norm(qf.reshape(B, L, N_FOX_HEADS, HEAD_DIM), q_norm_fox[l])
        kf = rms_norm(kf.reshape(B, L, N_FOX_HEADS, HEAD_DIM), k_norm_fox[l])
        vf = vf.reshape(B, L, N_FOX_HEADS, HEAD_DIM)
        logf = jax.nn.log_sigmoid(fgate.astype(jnp.float32) + b_forget[l].astype(jnp.float32))
        y_fox = fox_attention(qf, kf, vf, logf).reshape(B, L, FOX_W) @ w_out_fox[l]

        ckv = rms_norm(ckv, kv_norm_g[l])
        kd, vd = jnp.split(ckv @ w_ukv[l], 2, axis=-1)
        qd = rms_norm(qd.reshape(B, L, N_DSA_HEADS, HEAD_DIM), q_norm_dsa[l])
        kd = rms_norm(kd.reshape(B, L, N_DSA_HEADS, HEAD_DIM), k_norm_dsa[l])
        vd = vd.reshape(B, L, N_DSA_HEADS, HEAD_DIM)
        qi = qi.reshape(B, L, N_IDX_HEADS, IDX_DIM)
        wi = wi * (N_IDX_HEADS ** -0.5)
        y_dsa = dsa_attention(qd, kd, vd, qi, ki, wi, rel_bias, topk).reshape(B, L, DSA_W) @ w_out_dsa[l]

        merged = jax.nn.sigmoid(ga) * y_fox + jax.nn.sigmoid(gb) * y_dsa
        x = x + g1 * (merged @ w_out[l])

        h = rms_norm(x, norm2_g[l]) * (1 + sc2) + sh2
        u = causal_dwconv(h @ w_ffn_in[l], conv_w[l], conv_b[l])
        a, b = jnp.split(u, 2, axis=-1)
        x = x + g2 * ((jax.nn.silu(a) * b) @ w_ffn_out[l])
    return x
```

```python
import functools
import math

import jax
import jax.numpy as jnp
from jax import lax
from jax.experimental import pallas as pl
from jax.experimental.pallas import tpu as pltpu

HEAD_DIM = 128
N_FOX_HEADS = 8
N_DSA_HEADS = 8
FOX_W = N_FOX_HEADS * HEAD_DIM
DSA_W = N_DSA_HEADS * HEAD_DIM
KV_LORA = 256
N_IDX_HEADS = 16
IDX_DIM = 64
TOPK_MAX = 256
N_BUCKETS = 32
MAX_DISTANCE = 128
CONV_WIDTH = 3
EPS = 1e-6
NEG_INF = -1e30

LANES = 128
INT_MIN = -(2 ** 31)
VMEM_LIMIT = 56 * 1024 * 1024

F32 = jnp.float32
BF16 = jnp.bfloat16

MISC_FG = 0
MISC_WI = N_FOX_HEADS
MISC_KI = MISC_WI + N_IDX_HEADS


def _cparams(sem):
    return pltpu.CompilerParams(dimension_semantics=sem, vmem_limit_bytes=VMEM_LIMIT)


def _ada_kernel(c_ref, w_ref, b_ref, o_ref):
    c = c_ref[...]
    ca = c * jax.nn.sigmoid(c)
    o_ref[...] = jnp.dot(ca, w_ref[...], preferred_element_type=F32,
                         precision=lax.Precision.HIGHEST) + b_ref[...]


def _ada(c8, w, b, tn=1024):
    rows, d = c8.shape
    n = w.shape[1]
    return pl.pallas_call(
        _ada_kernel,
        out_shape=jax.ShapeDtypeStruct((rows, n), F32),
        grid=(n // tn,),
        in_specs=[pl.BlockSpec((rows, d), lambda j: (0, 0)),
                  pl.BlockSpec((d, tn), lambda j: (0, j)),
                  pl.BlockSpec((1, tn), lambda j: (0, j))],
        out_specs=pl.BlockSpec((rows, tn), lambda j: (0, j)),
        compiler_params=_cparams(("arbitrary",)),
        name="ada",
    )(c8, w, b)


def _modulated_norm(x, g, sc, sh):
    ms = jnp.mean(x * x, axis=-1, keepdims=True)
    return (x * lax.rsqrt(ms + EPS) * g) * (1.0 + sc) + sh


def _normmod_kernel(x_ref, g_ref, mod_ref, o_ref):
    o_ref[...] = _modulated_norm(x_ref[...], g_ref[...], mod_ref[1:2, :], mod_ref[0:1, :]).astype(o_ref.dtype)


def _normmod(x2, g, mod3, seq, tm=512):
    m, d = x2.shape
    per_b = seq // tm
    return pl.pallas_call(
        _normmod_kernel,
        out_shape=jax.ShapeDtypeStruct((m, d), BF16),
        grid=(m // tm,),
        in_specs=[pl.BlockSpec((tm, d), lambda i: (i, 0)),
                  pl.BlockSpec((1, d), lambda i: (0, 0)),
                  pl.BlockSpec((None, 6, d), lambda i: (i // per_b, 0, 0))],
        out_specs=pl.BlockSpec((tm, d), lambda i: (i, 0)),
        compiler_params=_cparams(("parallel",)),
        name="normmod",
    )(x2, g, mod3)


def _head_norm_store(acc, gain_ref, o_ref, col0=0):
    for hh in range(acc.shape[1] // HEAD_DIM):
        a = acc[:, hh * HEAD_DIM:(hh + 1) * HEAD_DIM]
        ms = jnp.mean(a * a, axis=-1, keepdims=True)
        sl = slice(col0 + hh * HEAD_DIM, col0 + (hh + 1) * HEAD_DIM)
        o_ref[:, sl] = (a * lax.rsqrt(ms + EPS) * gain_ref[:, hh * HEAD_DIM:(hh + 1) * HEAD_DIM]).astype(o_ref.dtype)


def _proj_kernel(h_ref, w_ref, gain_ref, o_ref, *, mode):
    acc = jnp.dot(h_ref[...], w_ref[...], preferred_element_type=F32)
    if mode == "headnorm":
        _head_norm_store(acc, gain_ref, o_ref)
    elif mode == "sigmoid":
        o_ref[...] = jax.nn.sigmoid(acc).astype(o_ref.dtype)
    else:
        o_ref[...] = acc.astype(o_ref.dtype)


def _proj(h, w, gain, *, mode, out_dtype, tm=512, tn=1024, name="proj"):
    m, d = h.shape
    n = w.shape[1]
    tn = min(tn, n)
    if gain is None:
        gain = jnp.ones((1, n), F32)
    return pl.pallas_call(
        functools.partial(_proj_kernel, mode=mode),
        out_shape=jax.ShapeDtypeStruct((m, n), out_dtype),
        grid=(n // tn, m // tm),
        in_specs=[pl.BlockSpec((tm, d), lambda j, i: (i, 0)),
                  pl.BlockSpec((d, tn), lambda j, i: (0, j)),
                  pl.BlockSpec((1, tn), lambda j, i: (0, j))],
        out_specs=pl.BlockSpec((tm, tn), lambda j, i: (i, j)),
        compiler_params=_cparams(("parallel", "parallel")),
        name=name,
    )(h, w, gain)


def _ckv_kernel(h_ref, wc_ref, g_ref, wu_ref, gk_ref, k_ref, v_ref):
    c = jnp.dot(h_ref[...], wc_ref[...], preferred_element_type=F32)
    ms = jnp.mean(c * c, axis=-1, keepdims=True)
    cn = (c * lax.rsqrt(ms + EPS) * g_ref[...]).astype(BF16)
    kv = jnp.dot(cn, wu_ref[...], preferred_element_type=F32)
    _head_norm_store(kv[:, :DSA_W], gk_ref, k_ref)
    v_ref[...] = kv[:, DSA_W:].astype(v_ref.dtype)


def _ckv(h, wc, g, wu, gk, tm=512):
    m, d = h.shape
    return pl.pallas_call(
        _ckv_kernel,
        out_shape=(jax.ShapeDtypeStruct((m, DSA_W), BF16), jax.ShapeDtypeStruct((m, DSA_W), BF16)),
        grid=(m // tm,),
        in_specs=[pl.BlockSpec((tm, d), lambda i: (i, 0)),
                  pl.BlockSpec((d, KV_LORA), lambda i: (0, 0)),
                  pl.BlockSpec((1, KV_LORA), lambda i: (0, 0)),
                  pl.BlockSpec((KV_LORA, 2 * DSA_W), lambda i: (0, 0)),
                  pl.BlockSpec((1, DSA_W), lambda i: (0, 0))],
        out_specs=(pl.BlockSpec((tm, DSA_W), lambda i: (i, 0)), pl.BlockSpec((tm, DSA_W), lambda i: (i, 0))),
        compiler_params=_cparams(("parallel",)),
        name="ckv",
    )(h, wc, g, wu, gk)


def _cum_kernel(m_ref, bf_ref, o_ref, carry_ref):
    @pl.when(pl.program_id(1) == 0)
    def _():
        carry_ref[...] = jnp.zeros_like(carry_ref)

    z = m_ref[...] + bf_ref[...]
    lf = -(jnp.maximum(-z, 0.0) + jnp.log1p(jnp.exp(-jnp.abs(z))))
    tc = lf.shape[0]
    row = lax.broadcasted_iota(jnp.int32, lf.shape, 0)
    s = 1
    while s < tc:
        lf = lf + jnp.where(row >= s, pltpu.roll(lf, s, axis=0), 0.0)
        s *= 2
    out = lf + carry_ref[0:1, :]
    o_ref[...] = out
    carry_ref[...] = jnp.broadcast_to(out[tc - 1:tc, :], carry_ref.shape)


def _cumsum(misc3, bf, tc=512):
    b, seq, w = misc3.shape
    return pl.pallas_call(
        _cum_kernel,
        out_shape=jax.ShapeDtypeStruct((b, seq, w), F32),
        grid=(b, seq // tc),
        in_specs=[pl.BlockSpec((None, tc, w), lambda bb, i: (bb, i, 0)),
                  pl.BlockSpec((1, w), lambda bb, i: (0, 0))],
        out_specs=pl.BlockSpec((None, tc, w), lambda bb, i: (bb, i, 0)),
        scratch_shapes=[pltpu.VMEM((8, w), F32)],
        compiler_params=_cparams(("arbitrary", "arbitrary")),
        name="cumsum",
    )(misc3, bf)


def _softmax_update(s, v_h, m_scr, l_scr, acc_ref, h):
    m_prev = m_scr[h]
    m_new = jnp.maximum(m_prev, jnp.max(s, axis=1, keepdims=True))
    alpha = jnp.exp(m_prev - m_new)
    p = jnp.exp(s - m_new[:, :1])
    l_scr[h] = alpha * l_scr[h] + jnp.sum(p, axis=1, keepdims=True)
    m_scr[h] = m_new
    sl = slice(h * HEAD_DIM, (h + 1) * HEAD_DIM)
    acc_ref[:, sl] = acc_ref[:, sl] * alpha + jnp.dot(p.astype(BF16), v_h, preferred_element_type=F32)


def _attn_init(m_scr, l_scr, acc_ref):
    m_scr[...] = jnp.full(m_scr.shape, NEG_INF, F32)
    l_scr[...] = jnp.zeros(l_scr.shape, F32)
    acc_ref[...] = jnp.zeros(acc_ref.shape, F32)


def _attn_finish(o_ref, l_scr, acc_ref, nheads):
    for h in range(nheads):
        sl = slice(h * HEAD_DIM, (h + 1) * HEAD_DIM)
        o_ref[:, sl] = (acc_ref[:, sl] / l_scr[h]).astype(o_ref.dtype)


def _qk(q_ref, k_ref, h):
    sl = slice(h * HEAD_DIM, (h + 1) * HEAD_DIM)
    return lax.dot_general(q_ref[:, sl], k_ref[:, sl], (((1,), (1,)), ((), ())), preferred_element_type=F32)


def _fox_kernel(q_ref, k_ref, v_ref, cq_ref, ck_ref, o_ref, m_scr, l_scr, acc_ref):
    i = pl.program_id(1)
    j = pl.program_id(2)
    tq, tk = q_ref.shape[0], k_ref.shape[0]

    @pl.when(j == 0)
    def _():
        _attn_init(m_scr, l_scr, acc_ref)

    def tile(masked):
        if masked:
            keep = (lax.broadcasted_iota(jnp.int32, (tq, tk), 1) <= lax.broadcasted_iota(jnp.int32, (tq, tk), 0))
        for h in range(N_FOX_HEADS):
            s = _qk(q_ref, k_ref, h) + cq_ref[:, h:h + 1] - ck_ref[h:h + 1, :]
            if masked:
                s = jnp.where(keep, s, NEG_INF)
            _softmax_update(s, v_ref[:, h * HEAD_DIM:(h + 1) * HEAD_DIM], m_scr, l_scr, acc_ref, h)

    @pl.when(j < i)
    def _():
        tile(False)

    @pl.when(j == i)
    def _():
        tile(True)
        _attn_finish(o_ref, l_scr, acc_ref, N_FOX_HEADS)


def _fox(q, k, v, cum, cum_t, t=512):
    b, seq, w = q.shape
    n = seq // t
    kv_map = lambda bb, i, j: (bb, jnp.minimum(j, i), 0)
    return pl.pallas_call(
        _fox_kernel,
        out_shape=jax.ShapeDtypeStruct((b, seq, w), BF16),
        grid=(b, n, n),
        in_specs=[pl.BlockSpec((None, t, w), lambda bb, i, j: (bb, i, 0)),
                  pl.BlockSpec((None, t, w), kv_map),
                  pl.BlockSpec((None, t, w), kv_map),
                  pl.BlockSpec((None, t, LANES), lambda bb, i, j: (bb, i, 0)),
                  pl.BlockSpec((None, N_FOX_HEADS, t), lambda bb, i, j: (bb, 0, jnp.minimum(j, i)))],
        out_specs=pl.BlockSpec((None, t, w), lambda bb, i, j: (bb, i, 0)),
        scratch_shapes=[pltpu.VMEM((N_FOX_HEADS, t, LANES), F32),
                        pltpu.VMEM((N_FOX_HEADS, t, LANES), F32),
                        pltpu.VMEM((t, w), F32)],
        compiler_params=_cparams(("parallel", "parallel", "arbitrary")),
        name="fox",
    )(q, k, v, cum, cum_t)


IDX_ROWS = 128
IDX_CH = 256


def _index_kernel(qi_ref, misc_ref, kit_ref, o_ref, keys_scr, wb_scr, *, topk):
    i = pl.program_id(1)
    seq = kit_ref.shape[1]
    nsub = IDX_CH // LANES
    t0 = i * IDX_ROWS
    nch = (t0 + IDX_ROWS + IDX_CH - 1) // IDX_CH
    w = misc_ref[:, MISC_WI:MISC_WI + N_IDX_HEADS] * (N_IDX_HEADS ** -0.5 * IDX_DIM ** -0.5)
    for jh in range(N_IDX_HEADS):
        wb_scr[jh] = jnp.broadcast_to(w[:, jh:jh + 1], (IDX_ROWS, LANES))
    row = t0 + lax.broadcasted_iota(jnp.int32, (IDX_ROWS, LANES), 0)
    lane = lax.broadcasted_iota(jnp.int32, (IDX_ROWS, LANES), 1)

    def score_chunk(c, carry):
        c0 = pl.multiple_of(c * IDX_CH, IDX_CH)
        kc = kit_ref[:, pl.ds(c0, IDX_CH)]
        z = jnp.zeros_like(kc)
        rhs = jnp.concatenate([jnp.concatenate([kc, z], axis=1), jnp.concatenate([z, kc], axis=1)], axis=0)
        accs = [jnp.zeros((IDX_ROWS, LANES), F32) for _ in range(nsub)]
        for p in range(N_IDX_HEADS // 2):
            s2 = jnp.dot(qi_ref[:, p * 2 * IDX_DIM:(p + 1) * 2 * IDX_DIM], rhs, preferred_element_type=F32)
            wa, wb = wb_scr[2 * p], wb_scr[2 * p + 1]
            for u in range(nsub):
                accs[u] = (accs[u] + wa * jnp.maximum(s2[:, u * LANES:(u + 1) * LANES], 0.0)
                           + wb * jnp.maximum(s2[:, IDX_CH + u * LANES:IDX_CH + (u + 1) * LANES], 0.0))
        for u in range(nsub):
            bits = pltpu.bitcast(accs[u], jnp.int32)
            key = bits ^ ((bits >> 31) & 0x7FFFFFFF)
            col = c0 + u * LANES + lane
            keys_scr[:, pl.ds(pl.multiple_of(c0 + u * LANES, LANES), LANES)] = jnp.where(col <= row, key, INT_MIN)
        return carry

    lax.fori_loop(0, nch, score_chunk, 0)

    def count_ge(cand):
        def body(c, acc):
            c0 = pl.multiple_of(c * IDX_CH, IDX_CH)
            blk = keys_scr[:, pl.ds(c0, IDX_CH)]
            for u in range(nsub):
                acc = acc + (blk[:, u * LANES:(u + 1) * LANES] >= cand).astype(jnp.int32)
            return acc
        acc = lax.fori_loop(0, nch, body, jnp.zeros((IDX_ROWS, LANES), jnp.int32))
        return jnp.broadcast_to(jnp.sum(acc, axis=1, keepdims=True), (IDX_ROWS, LANES))

    def bit_step(t, ans):
        cand = ans ^ (jnp.int32(1) << (31 - t))
        return jnp.where(count_ge(cand) >= topk, cand, ans)

    ans = lax.fori_loop(0, 32, bit_step, jnp.full((IDX_ROWS, LANES), INT_MIN, jnp.int32))
    thr = jnp.maximum(ans, INT_MIN + 1)

    def emit(c, carry):
        c0 = pl.multiple_of(c * IDX_CH, IDX_CH)
        blk = keys_scr[:, pl.ds(c0, IDX_CH)]
        for u in range(nsub):
            sel = blk[:, u * LANES:(u + 1) * LANES] >= thr
            o_ref[:, pl.ds(pl.multiple_of(c0 + u * LANES, LANES), LANES)] = jnp.where(sel, 0.0, NEG_INF).astype(o_ref.dtype)
        return carry

    lax.fori_loop(0, nch, emit, 0)

    def fill(c, carry):
        c0 = pl.multiple_of(c * IDX_CH, IDX_CH)
        o_ref[:, pl.ds(c0, IDX_CH)] = jnp.full((IDX_ROWS, IDX_CH), NEG_INF, o_ref.dtype)
        return carry

    lax.fori_loop(nch, seq // IDX_CH, fill, 0)


def _index(qi, misc3, kit, topk):
    b, seq, w = qi.shape
    return pl.pallas_call(
        functools.partial(_index_kernel, topk=topk),
        out_shape=jax.ShapeDtypeStruct((b, seq, seq), BF16),
        grid=(b, seq // IDX_ROWS),
        in_specs=[pl.BlockSpec((None, IDX_ROWS, w), lambda bb, i: (bb, i, 0)),
                  pl.BlockSpec((None, IDX_ROWS, LANES), lambda bb, i: (bb, i, 0)),
                  pl.BlockSpec((None, IDX_DIM, seq), lambda bb, i: (bb, 0, 0))],
        out_specs=pl.BlockSpec((None, IDX_ROWS, seq), lambda bb, i: (bb, i, 0)),
        scratch_shapes=[pltpu.VMEM((IDX_ROWS, seq), jnp.int32),
                        pltpu.VMEM((N_IDX_HEADS, IDX_ROWS, LANES), F32)],
        compiler_params=_cparams(("parallel", "arbitrary")),
        name="index",
    )(qi, misc3, kit)


def _dsa_kernel(tab_ref, q_ref, k_ref, v_ref, b_ref, o_ref, m_scr, l_scr, acc_ref, toe_scr, s_scr):
    i = pl.program_id(1)
    j = pl.program_id(2)
    tq, tk = q_ref.shape[0], k_ref.shape[0]
    nb = tq // LANES

    @pl.when(j == 0)
    def _():
        _attn_init(m_scr, l_scr, acc_ref)

    @pl.when((i == 0) & (j == 0))
    def _():
        r = lax.broadcasted_iota(jnp.int32, (LANES, LANES), 0)
        c = lax.broadcasted_iota(jnp.int32, (LANES, LANES), 1)
        d_diag, d_sub = r - c, LANES + r - c
        for h in range(N_DSA_HEADS):
            def body(d, carry, h=h):
                td, ts = carry
                val = tab_ref[h, d]
                return jnp.where(d_diag == d, val, td), jnp.where(d_sub == d, val, ts)
            zero = jnp.zeros((LANES, LANES), F32)
            td, ts = lax.fori_loop(0, LANES, body, (zero, zero))
            toe_scr[h, 0] = td
            toe_scr[h, 1] = ts

    def tile(kind):
        bias = b_ref[...].astype(F32)
        for h in range(N_DSA_HEADS):
            s = _qk(q_ref, k_ref, h) + bias
            if kind != "far":
                s_scr[...] = s
                if kind == "diag":
                    for a in range(nb):
                        sl = slice(a * LANES, (a + 1) * LANES)
                        s_scr[sl, sl] += toe_scr[h, 0]
                    for a in range(1, nb):
                        s_scr[a * LANES:(a + 1) * LANES, (a - 1) * LANES:a * LANES] += toe_scr[h, 1]
                else:
                    s_scr[0:LANES, (nb - 1) * LANES:nb * LANES] += toe_scr[h, 1]
                s = s_scr[...]
            _softmax_update(s, v_ref[:, h * HEAD_DIM:(h + 1) * HEAD_DIM], m_scr, l_scr, acc_ref, h)

    @pl.when(j < i - 1)
    def _():
        tile("far")

    @pl.when(j == i - 1)
    def _():
        tile("near")

    @pl.when(j == i)
    def _():
        tile("diag")
        _attn_finish(o_ref, l_scr, acc_ref, N_DSA_HEADS)


def _dsa(tab, q, k, v, bias, t=512):
    b, seq, w = q.shape
    n = seq // t
    kv_map = lambda bb, i, j: (bb, jnp.minimum(j, i), 0)
    return pl.pallas_call(
        _dsa_kernel,
        out_shape=jax.ShapeDtypeStruct((b, seq, w), BF16),
        grid=(b, n, n),
        in_specs=[pl.BlockSpec(memory_space=pltpu.SMEM),
                  pl.BlockSpec((None, t, w), lambda bb, i, j: (bb, i, 0)),
                  pl.BlockSpec((None, t, w), kv_map),
                  pl.BlockSpec((None, t, w), kv_map),
                  pl.BlockSpec((None, t, t), lambda bb, i, j: (bb, i, jnp.minimum(j, i)))],
        out_specs=pl.BlockSpec((None, t, w), lambda bb, i, j: (bb, i, 0)),
        scratch_shapes=[pltpu.VMEM((N_DSA_HEADS, t, LANES), F32),
                        pltpu.VMEM((N_DSA_HEADS, t, LANES), F32),
                        pltpu.VMEM((t, w), F32),
                        pltpu.VMEM((N_DSA_HEADS, 2, LANES, LANES), F32),
                        pltpu.VMEM((t, t), F32)],
        compiler_params=_cparams(("arbitrary", "arbitrary", "arbitrary")),
        name="dsa",
    )(tab, q, k, v, bias)


def _merge_kernel(af_ref, ad_ref, ga_ref, gb_ref, x_ref, mod_ref, wof_ref, wod_ref, wo_ref, g2_ref,
                  x1_ref, h2_ref):
    yf = jnp.dot(af_ref[...], wof_ref[...], preferred_element_type=F32)
    yd = jnp.dot(ad_ref[...], wod_ref[...], preferred_element_type=F32)
    merged = ga_ref[...].astype(F32) * yf + gb_ref[...].astype(F32) * yd
    o = jnp.dot(merged.astype(BF16), wo_ref[...], preferred_element_type=F32)
    x1 = x_ref[...] + mod_ref[2:3, :] * o
    x1_ref[...] = x1
    h2_ref[...] = _modulated_norm(x1, g2_ref[...], mod_ref[4:5, :], mod_ref[3:4, :]).astype(h2_ref.dtype)


def _merge(af, ad, ga, gb, x2, mod3, wof, wod, wo, g2, seq, tm=256):
    m, d = x2.shape
    per_b = seq // tm
    row = lambda i: (i, 0)
    const = lambda i: (0, 0)
    return pl.pallas_call(
        _merge_kernel,
        out_shape=(jax.ShapeDtypeStruct((m, d), F32), jax.ShapeDtypeStruct((m, d), BF16)),
        grid=(m // tm,),
        in_specs=[pl.BlockSpec((tm, FOX_W), row), pl.BlockSpec((tm, DSA_W), row),
                  pl.BlockSpec((tm, d), row), pl.BlockSpec((tm, d), lambda i: (i, 1)), pl.BlockSpec((tm, d), row),
                  pl.BlockSpec((None, 6, d), lambda i: (i // per_b, 0, 0)),
                  pl.BlockSpec((FOX_W, d), const), pl.BlockSpec((DSA_W, d), const), pl.BlockSpec((d, d), const),
                  pl.BlockSpec((1, d), const)],
        out_specs=(pl.BlockSpec((tm, d), row), pl.BlockSpec((tm, d), row)),
        compiler_params=_cparams(("parallel",)),
        name="merge",
    )(af, ad, ga, gb, x2, mod3, wof, wod, wo, g2)


FFN_HALO = 16


def _ffn_kernel(h_ref, halo_ref, wa_ref, wb_ref, cwa_ref, cwb_ref, cba_ref, cbb_ref, wout_ref, x1_ref, mod_ref,
                o_ref, hext_scr, acc_ref, *, per_b):
    i = pl.program_id(0)
    f = pl.program_id(1)
    tm = h_ref.shape[0]

    @pl.when(f == 0)
    def _():
        first = (i % per_b) == 0
        hext_scr[0:FFN_HALO, :] = jnp.where(first, jnp.zeros_like(halo_ref[...]), halo_ref[...])
        hext_scr[FFN_HALO:, :] = h_ref[...]
        acc_ref[...] = jnp.zeros_like(acc_ref)

    hext = hext_scr[...]

    def conv(w_ref, cw_ref, cb_ref):
        u = jnp.dot(hext, w_ref[...], preferred_element_type=F32)
        y = cw_ref[2:3, :] * u + cw_ref[1:2, :] * pltpu.roll(u, 1, axis=0) + cw_ref[0:1, :] * pltpu.roll(u, 2, axis=0)
        return y[FFN_HALO:, :] + cb_ref[...]

    ya = conv(wa_ref, cwa_ref, cba_ref)
    yb = conv(wb_ref, cwb_ref, cbb_ref)
    act = (ya * jax.nn.sigmoid(ya) * yb).astype(BF16)
    acc_ref[...] += jnp.dot(act, wout_ref[...], preferred_element_type=F32)

    @pl.when(f == pl.num_programs(1) - 1)
    def _():
        o_ref[...] = x1_ref[...] + mod_ref[5:6, :] * acc_ref[...]


def _ffn(h2, w_in, conv_w, conv_b, w_out, x1, mod3, seq, tm=512, tf=512):
    m, d = h2.shape
    dff = w_out.shape[0]
    nf = dff // tf
    per_b = seq // tm
    hb = tm // FFN_HALO
    return pl.pallas_call(
        functools.partial(_ffn_kernel, per_b=per_b),
        out_shape=jax.ShapeDtypeStruct((m, d), F32),
        grid=(m // tm, nf),
        in_specs=[pl.BlockSpec((tm, d), lambda i, f: (i, 0)),
                  pl.BlockSpec((FFN_HALO, d), lambda i, f: (jnp.maximum(i * hb - 1, 0), 0)),
                  pl.BlockSpec((d, tf), lambda i, f: (0, f)),
                  pl.BlockSpec((d, tf), lambda i, f: (0, f + nf)),
                  pl.BlockSpec((CONV_WIDTH, tf), lambda i, f: (0, f)),
                  pl.BlockSpec((CONV_WIDTH, tf), lambda i, f: (0, f + nf)),
                  pl.BlockSpec((1, tf), lambda i, f: (0, f)),
                  pl.BlockSpec((1, tf), lambda i, f: (0, f + nf)),
                  pl.BlockSpec((tf, d), lambda i, f: (f, 0)),
                  pl.BlockSpec((tm, d), lambda i, f: (i, 0)),
                  pl.BlockSpec((None, 6, d), lambda i, f: (i // per_b, 0, 0))],
        out_specs=pl.BlockSpec((tm, d), lambda i, f: (i, 0)),
        scratch_shapes=[pltpu.VMEM((tm + FFN_HALO, d), BF16), pltpu.VMEM((tm, d), F32)],
        compiler_params=_cparams(("parallel", "arbitrary")),
        name="ffn",
    )(h2, h2, w_in, w_in, conv_w, conv_w, conv_b, conv_b, w_out, x1, mod3)


def _t5_bucket(n):
    n = jnp.maximum(n, 0)
    max_exact = N_BUCKETS // 2
    nf = jnp.maximum(n, 1).astype(F32)
    large = max_exact + (jnp.log(nf / max_exact) / math.log(MAX_DISTANCE / max_exact)
                         * (N_BUCKETS - max_exact)).astype(jnp.int32)
    large = jnp.minimum(large, N_BUCKETS - 1)
    return jnp.where(n < max_exact, n, large)


def _layer(x, c8, w_ada, b_ada, norm1_g, w_in, b_forget, q_norm_fox, k_norm_fox, kv_norm_g, w_ukv, q_norm_dsa,
           k_norm_dsa, w_out_fox, w_out_dsa, w_out, norm2_g, w_ffn_in, conv_w, conv_b, w_ffn_out, rel_bias):
    b, seq, d = x.shape
    m = b * seq
    topk = min(TOPK_MAX, seq // 4)
    x2 = x.reshape(m, d)

    mod3 = _ada(c8, w_ada, b_ada.reshape(1, -1))[:b].reshape(b, 6, d)
    h1 = _normmod(x2, norm1_g.reshape(1, d), mod3, seq)

    o = 0
    cols = {}
    for name, size in (("qf", FOX_W), ("kf", FOX_W), ("vf", FOX_W), ("fg", N_FOX_HEADS), ("qd", DSA_W),
                       ("ckv", KV_LORA), ("qi", N_IDX_HEADS * IDX_DIM), ("ki", IDX_DIM), ("wi", N_IDX_HEADS),
                       ("ga", d), ("gb", d)):
        cols[name] = w_in[:, o:o + size]
        o += size
    wb = lambda a: a.astype(BF16)
    scale = HEAD_DIM ** -0.5
    tile_h = lambda g, nh: jnp.tile(g.reshape(1, HEAD_DIM), (1, nh))

    qf = _proj(h1, wb(cols["qf"]), tile_h(q_norm_fox, N_FOX_HEADS) * scale, mode="headnorm", out_dtype=BF16, name="proj_qf")
    kf = _proj(h1, wb(cols["kf"]), tile_h(k_norm_fox, N_FOX_HEADS), mode="headnorm", out_dtype=BF16, name="proj_kf")
    vf = _proj(h1, wb(cols["vf"]), None, mode="plain", out_dtype=BF16, name="proj_vf")
    qd = _proj(h1, wb(cols["qd"]), tile_h(q_norm_dsa, N_DSA_HEADS) * scale, mode="headnorm", out_dtype=BF16, name="proj_qd")
    qi = _proj(h1, wb(cols["qi"]), None, mode="plain", out_dtype=BF16, name="proj_qi")
    gates = _proj(h1, wb(jnp.concatenate([cols["ga"], cols["gb"]], axis=1)), None, mode="sigmoid", out_dtype=BF16,
                  name="proj_gates")
    pad = jnp.zeros((d, LANES - N_FOX_HEADS - N_IDX_HEADS - IDX_DIM), w_in.dtype)
    misc = _proj(h1, wb(jnp.concatenate([cols["fg"], cols["wi"], cols["ki"], pad], axis=1)), None, mode="plain",
                 out_dtype=F32, name="proj_misc")
    kd, vd = _ckv(h1, wb(cols["ckv"]), kv_norm_g.reshape(1, KV_LORA), wb(w_ukv), tile_h(k_norm_dsa, N_DSA_HEADS))

    misc3 = misc.reshape(b, seq, LANES)
    bf = jnp.zeros((1, LANES), F32).at[0, :N_FOX_HEADS].set(b_forget.astype(F32))
    cum = _cumsum(misc3, bf)
    cum_t = jnp.swapaxes(cum[:, :, :N_FOX_HEADS], 1, 2)
    r3 = lambda a: a.reshape(b, seq, -1)
    a_fox = _fox(r3(qf), r3(kf), r3(vf), cum, cum_t)

    kit = jnp.swapaxes(misc3[:, :, MISC_KI:MISC_KI + IDX_DIM], 1, 2).astype(BF16)
    sel_bias = _index(r3(qi), misc3, kit, topk)
    by_dist = rel_bias[_t5_bucket(jnp.arange(LANES, dtype=jnp.int32))] - rel_bias[N_BUCKETS - 1][None, :]
    a_dsa = _dsa(by_dist.T.astype(F32), r3(qd), r3(kd), r3(vd), sel_bias)

    x1, h2 = _merge(a_fox.reshape(m, FOX_W), a_dsa.reshape(m, DSA_W), gates, gates, x2, mod3,
                    wb(w_out_fox), wb(w_out_dsa), wb(w_out), norm2_g.reshape(1, d), seq)
    out = _ffn(h2, wb(w_ffn_in), conv_w, conv_b.reshape(1, -1), wb(w_ffn_out), x1, mod3, seq)
    return out.reshape(b, seq, d)


def kernel(x, c, w_ada, b_ada, norm1_g, w_in, b_forget, q_norm_fox, k_norm_fox, kv_norm_g, w_ukv, q_norm_dsa, k_norm_dsa, w_out_fox, w_out_dsa, w_out, norm2_g, w_ffn_in, conv_w, conv_b, w_ffn_out, rel_bias):
    b = x.shape[0]
    c8 = jnp.zeros((8, c.shape[1]), c.dtype).at[:b].set(c)
    for l in range(w_ada.shape[0]):
        x = _layer(x, c8, w_ada[l], b_ada[l], norm1_g[l], w_in[l], b_forget[l], q_norm_fox[l], k_norm_fox[l],
                   kv_norm_g[l], w_ukv[l], q_norm_dsa[l], k_norm_dsa[l], w_out_fox[l], w_out_dsa[l], w_out[l],
                   norm2_g[l], w_ffn_in[l], conv_w[l], conv_b[l], w_ffn_out[l], rel_bias)
    return x
```

```python
import functools
import math

import jax
import jax.numpy as jnp
from jax import lax
from jax.experimental import pallas as pl
from jax.experimental.pallas import tpu as pltpu

HEAD_DIM = 128
N_FOX_HEADS = 8
N_DSA_HEADS = 8
FOX_W = N_FOX_HEADS * HEAD_DIM
DSA_W = N_DSA_HEADS * HEAD_DIM
KV_LORA = 256
N_IDX_HEADS = 16
IDX_DIM = 64
TOPK_MAX = 256
N_BUCKETS = 32
MAX_DISTANCE = 128
CONV_WIDTH = 3
EPS = 1e-6
NEG_INF = -1e30
LOG2E = 1.4426950408889634

LANES = 128
INT_MIN = -(2 ** 31)
VMEM_LIMIT = 56 * 1024 * 1024

F32 = jnp.float32
BF16 = jnp.bfloat16

MISC_FG = 0
MISC_WI = N_FOX_HEADS
MISC_KI = MISC_WI + N_IDX_HEADS


def _cparams(sem):
    return pltpu.CompilerParams(dimension_semantics=sem, vmem_limit_bytes=VMEM_LIMIT)


def _ada_kernel(c_ref, w_ref, b_ref, o_ref):
    c = c_ref[...]
    ca = c * jax.nn.sigmoid(c)
    o_ref[...] = jnp.dot(ca, w_ref[...], preferred_element_type=F32,
                         precision=lax.Precision.HIGHEST) + b_ref[...]


def _ada(c8, w, b, tn=1024):
    rows, d = c8.shape
    n = w.shape[1]
    return pl.pallas_call(
        _ada_kernel,
        out_shape=jax.ShapeDtypeStruct((rows, n), F32),
        grid=(n // tn,),
        in_specs=[pl.BlockSpec((rows, d), lambda j: (0, 0)),
                  pl.BlockSpec((d, tn), lambda j: (0, j)),
                  pl.BlockSpec((1, tn), lambda j: (0, j))],
        out_specs=pl.BlockSpec((rows, tn), lambda j: (0, j)),
        compiler_params=_cparams(("arbitrary",)),
        name="ada",
    )(c8, w, b)


def _modulated_norm(x, g, sc, sh):
    ms = jnp.mean(x * x, axis=-1, keepdims=True)
    return (x * lax.rsqrt(ms + EPS) * g) * (1.0 + sc) + sh


def _normmod_kernel(x_ref, g_ref, mod_ref, o_ref):
    o_ref[...] = _modulated_norm(x_ref[...], g_ref[...], mod_ref[1:2, :], mod_ref[0:1, :]).astype(o_ref.dtype)


def _normmod(x2, g, mod3, seq, tm=512):
    m, d = x2.shape
    per_b = seq // tm
    return pl.pallas_call(
        _normmod_kernel,
        out_shape=jax.ShapeDtypeStruct((m, d), BF16),
        grid=(m // tm,),
        in_specs=[pl.BlockSpec((tm, d), lambda i: (i, 0)),
                  pl.BlockSpec((1, d), lambda i: (0, 0)),
                  pl.BlockSpec((None, 6, d), lambda i: (i // per_b, 0, 0))],
        out_specs=pl.BlockSpec((tm, d), lambda i: (i, 0)),
        compiler_params=_cparams(("parallel",)),
        name="normmod",
    )(x2, g, mod3)


def _head_norm_store(acc, gain_ref, o_ref, col0=0):
    for hh in range(acc.shape[1] // HEAD_DIM):
        a = acc[:, hh * HEAD_DIM:(hh + 1) * HEAD_DIM]
        ms = jnp.mean(a * a, axis=-1, keepdims=True)
        sl = slice(col0 + hh * HEAD_DIM, col0 + (hh + 1) * HEAD_DIM)
        o_ref[:, sl] = (a * lax.rsqrt(ms + EPS) * gain_ref[:, hh * HEAD_DIM:(hh + 1) * HEAD_DIM]).astype(o_ref.dtype)


def _proj_kernel(h_ref, w_ref, gain_ref, o_ref, *, mode):
    acc = jnp.dot(h_ref[...], w_ref[...], preferred_element_type=F32)
    if mode == "headnorm":
        _head_norm_store(acc, gain_ref, o_ref)
    elif mode == "sigmoid":
        o_ref[...] = jax.nn.sigmoid(acc).astype(o_ref.dtype)
    else:
        o_ref[...] = acc.astype(o_ref.dtype)


def _proj(h, w, gain, *, mode, out_dtype, tm=512, tn=1024, name="proj"):
    m, d = h.shape
    n = w.shape[1]
    tn = min(tn, n)
    if gain is None:
        gain = jnp.ones((1, n), F32)
    return pl.pallas_call(
        functools.partial(_proj_kernel, mode=mode),
        out_shape=jax.ShapeDtypeStruct((m, n), out_dtype),
        grid=(n // tn, m // tm),
        in_specs=[pl.BlockSpec((tm, d), lambda j, i: (i, 0)),
                  pl.BlockSpec((d, tn), lambda j, i: (0, j)),
                  pl.BlockSpec((1, tn), lambda j, i: (0, j))],
        out_specs=pl.BlockSpec((tm, tn), lambda j, i: (i, j)),
        compiler_params=_cparams(("parallel", "parallel")),
        name=name,
    )(h, w, gain)


def _ckv_kernel(h_ref, wc_ref, g_ref, wu_ref, gk_ref, k_ref, v_ref):
    c = jnp.dot(h_ref[...], wc_ref[...], preferred_element_type=F32)
    ms = jnp.mean(c * c, axis=-1, keepdims=True)
    cn = (c * lax.rsqrt(ms + EPS) * g_ref[...]).astype(BF16)
    kv = jnp.dot(cn, wu_ref[...], preferred_element_type=F32)
    _head_norm_store(kv[:, :DSA_W], gk_ref, k_ref)
    v_ref[...] = kv[:, DSA_W:].astype(v_ref.dtype)


def _ckv(h, wc, g, wu, gk, tm=512):
    m, d = h.shape
    return pl.pallas_call(
        _ckv_kernel,
        out_shape=(jax.ShapeDtypeStruct((m, DSA_W), BF16), jax.ShapeDtypeStruct((m, DSA_W), BF16)),
        grid=(m // tm,),
        in_specs=[pl.BlockSpec((tm, d), lambda i: (i, 0)),
                  pl.BlockSpec((d, KV_LORA), lambda i: (0, 0)),
                  pl.BlockSpec((1, KV_LORA), lambda i: (0, 0)),
                  pl.BlockSpec((KV_LORA, 2 * DSA_W), lambda i: (0, 0)),
                  pl.BlockSpec((1, DSA_W), lambda i: (0, 0))],
        out_specs=(pl.BlockSpec((tm, DSA_W), lambda i: (i, 0)), pl.BlockSpec((tm, DSA_W), lambda i: (i, 0))),
        compiler_params=_cparams(("parallel",)),
        name="ckv",
    )(h, wc, g, wu, gk)


def _cum_kernel(m_ref, bf_ref, o_ref, carry_ref):
    @pl.when(pl.program_id(1) == 0)
    def _():
        carry_ref[...] = jnp.zeros_like(carry_ref)

    z = m_ref[...] + bf_ref[...]
    lf = -(jnp.maximum(-z, 0.0) + jnp.log1p(jnp.exp(-jnp.abs(z))))
    tc = lf.shape[0]
    row = lax.broadcasted_iota(jnp.int32, lf.shape, 0)
    s = 1
    while s < tc:
        lf = lf + jnp.where(row >= s, pltpu.roll(lf, s, axis=0), 0.0)
        s *= 2
    out = lf + carry_ref[0:1, :]
    o_ref[...] = out
    carry_ref[...] = jnp.broadcast_to(out[tc - 1:tc, :], carry_ref.shape)


def _cumsum(misc3, bf, tc=512):
    b, seq, w = misc3.shape
    return pl.pallas_call(
        _cum_kernel,
        out_shape=jax.ShapeDtypeStruct((b, seq, w), F32),
        grid=(b, seq // tc),
        in_specs=[pl.BlockSpec((None, tc, w), lambda bb, i: (bb, i, 0)),
                  pl.BlockSpec((1, w), lambda bb, i: (0, 0))],
        out_specs=pl.BlockSpec((None, tc, w), lambda bb, i: (bb, i, 0)),
        scratch_shapes=[pltpu.VMEM((8, w), F32)],
        compiler_params=_cparams(("arbitrary", "arbitrary")),
        name="cumsum",
    )(misc3, bf)


def _softmax_update(s, v_h, m_scr, l_scr, acc_ref, h):
    m_prev = m_scr[h]
    m_new = jnp.maximum(m_prev, jnp.max(s, axis=1, keepdims=True))
    alpha = jnp.exp2(m_prev - m_new)
    p = jnp.exp2(s - m_new[:, :1]).astype(BF16)
    v_ones = jnp.concatenate([v_h, jnp.ones_like(v_h)], axis=1)
    pv = jnp.dot(p, v_ones, preferred_element_type=F32)
    l_scr[h] = alpha * l_scr[h] + pv[:, HEAD_DIM:]
    m_scr[h] = m_new
    sl = slice(h * HEAD_DIM, (h + 1) * HEAD_DIM)
    acc_ref[:, sl] = acc_ref[:, sl] * alpha + pv[:, :HEAD_DIM]


def _attn_init(m_scr, l_scr, acc_ref):
    m_scr[...] = jnp.full(m_scr.shape, NEG_INF, F32)
    l_scr[...] = jnp.zeros(l_scr.shape, F32)
    acc_ref[...] = jnp.zeros(acc_ref.shape, F32)


def _attn_finish(o_ref, l_scr, acc_ref, nheads):
    for h in range(nheads):
        sl = slice(h * HEAD_DIM, (h + 1) * HEAD_DIM)
        o_ref[:, sl] = (acc_ref[:, sl] / l_scr[h]).astype(o_ref.dtype)


def _qk(q_ref, k_ref, h, width=HEAD_DIM):
    sl = slice(h * width, (h + 1) * width)
    return lax.dot_general(q_ref[:, sl], k_ref[:, sl], (((1,), (1,)), ((), ())), preferred_element_type=F32)


AUG_W = 2 * HEAD_DIM
N_PIECES = 3


def _aug_kernel(q_ref, k_ref, cum_ref, selq_ref, selk_ref, cq_ref, ck_ref, qa_ref, ka_ref):
    lane = lax.broadcasted_iota(jnp.int32, cum_ref.shape, 1)
    c = jnp.where(lane < N_FOX_HEADS, cum_ref[...] * LOG2E, 0.0)
    hi = c.astype(BF16).astype(F32)
    r1 = c - hi
    mid = r1.astype(BF16).astype(F32)
    lo = (r1 - mid).astype(BF16).astype(F32)
    pieces = (hi + pltpu.roll(mid, N_FOX_HEADS, axis=1) + pltpu.roll(lo, 2 * N_FOX_HEADS, axis=1)).astype(BF16)
    for h in range(N_FOX_HEADS):
        sl = slice(h * HEAD_DIM, (h + 1) * HEAD_DIM)
        qa_ref[:, h * AUG_W:h * AUG_W + HEAD_DIM] = q_ref[:, sl]
        ka_ref[:, h * AUG_W:h * AUG_W + HEAD_DIM] = k_ref[:, sl]
        eq = jnp.dot(pieces, selq_ref[h], preferred_element_type=F32) + cq_ref[...]
        ek = jnp.dot(pieces, selk_ref[h], preferred_element_type=F32) + ck_ref[...]
        qa_ref[:, h * AUG_W + HEAD_DIM:(h + 1) * AUG_W] = eq.astype(BF16)
        ka_ref[:, h * AUG_W + HEAD_DIM:(h + 1) * AUG_W] = ek.astype(BF16)


def _aug_tables():
    selq = [[[0.0] * LANES for _ in range(LANES)] for _ in range(N_FOX_HEADS)]
    selk = [[[0.0] * LANES for _ in range(LANES)] for _ in range(N_FOX_HEADS)]
    for h in range(N_FOX_HEADS):
        for p in range(N_PIECES):
            selq[h][p * N_FOX_HEADS + h][p] = 1.0
            selk[h][p * N_FOX_HEADS + h][N_PIECES + p] = -1.0
    cq = [[1.0 if N_PIECES <= c < 2 * N_PIECES else 0.0 for c in range(LANES)]]
    ck = [[1.0 if c < N_PIECES else 0.0 for c in range(LANES)]]
    return (jnp.array(selq, BF16), jnp.array(selk, BF16), jnp.array(cq, F32), jnp.array(ck, F32))


def _augment(q, k, cum2, tm=512):
    m, w = q.shape
    selq, selk, cq, ck = _aug_tables()
    row = lambda i: (i, 0)
    const3 = lambda i: (0, 0, 0)
    const2 = lambda i: (0, 0)
    wa = N_FOX_HEADS * AUG_W
    return pl.pallas_call(
        _aug_kernel,
        out_shape=(jax.ShapeDtypeStruct((m, wa), BF16), jax.ShapeDtypeStruct((m, wa), BF16)),
        grid=(m // tm,),
        in_specs=[pl.BlockSpec((tm, w), row), pl.BlockSpec((tm, w), row), pl.BlockSpec((tm, LANES), row),
                  pl.BlockSpec((N_FOX_HEADS, LANES, LANES), const3), pl.BlockSpec((N_FOX_HEADS, LANES, LANES), const3),
                  pl.BlockSpec((1, LANES), const2), pl.BlockSpec((1, LANES), const2)],
        out_specs=(pl.BlockSpec((tm, wa), row), pl.BlockSpec((tm, wa), row)),
        compiler_params=_cparams(("parallel",)),
        name="augment",
    )(q, k, cum2, selq, selk, cq, ck)


def _fox_kernel(q_ref, k_ref, v_ref, o_ref, m_scr, l_scr, acc_ref):
    i = pl.program_id(1)
    j = pl.program_id(2)
    tq, tk = q_ref.shape[0], k_ref.shape[0]

    @pl.when(j == 0)
    def _():
        _attn_init(m_scr, l_scr, acc_ref)

    def tile(masked):
        if masked:
            keep = (lax.broadcasted_iota(jnp.int32, (tq, tk), 1) <= lax.broadcasted_iota(jnp.int32, (tq, tk), 0))
        for h in range(N_FOX_HEADS):
            s = _qk(q_ref, k_ref, h, AUG_W)
            if masked:
                s = jnp.where(keep, s, NEG_INF)
            _softmax_update(s, v_ref[:, h * HEAD_DIM:(h + 1) * HEAD_DIM], m_scr, l_scr, acc_ref, h)

    @pl.when(j < i)
    def _():
        tile(False)

    @pl.when(j == i)
    def _():
        tile(True)
        _attn_finish(o_ref, l_scr, acc_ref, N_FOX_HEADS)


def _fox(qa, ka, v, t=512):
    b, seq, w = v.shape
    wa = qa.shape[2]
    n = seq // t
    kv_map = lambda bb, i, j: (bb, jnp.minimum(j, i), 0)
    return pl.pallas_call(
        _fox_kernel,
        out_shape=jax.ShapeDtypeStruct((b, seq, w), BF16),
        grid=(b, n, n),
        in_specs=[pl.BlockSpec((None, t, wa), lambda bb, i, j: (bb, i, 0)),
                  pl.BlockSpec((None, t, wa), kv_map),
                  pl.BlockSpec((None, t, w), kv_map)],
        out_specs=pl.BlockSpec((None, t, w), lambda bb, i, j: (bb, i, 0)),
        scratch_shapes=[pltpu.VMEM((N_FOX_HEADS, t, LANES), F32),
                        pltpu.VMEM((N_FOX_HEADS, t, LANES), F32),
                        pltpu.VMEM((t, w), F32)],
        compiler_params=_cparams(("parallel", "parallel", "arbitrary")),
        name="fox",
    )(qa, ka, v)


IDX_ROWS = 128
IDX_CH = 256


def _index_kernel(qi_ref, misc_ref, kit_ref, o_ref, keys_scr, wb_scr, *, topk):
    i = pl.program_id(1)
    seq = kit_ref.shape[1]
    nsub = IDX_CH // LANES
    t0 = i * IDX_ROWS
    nch = (t0 + IDX_ROWS + IDX_CH - 1) // IDX_CH
    w = misc_ref[:, MISC_WI:MISC_WI + N_IDX_HEADS] * (N_IDX_HEADS ** -0.5 * IDX_DIM ** -0.5)
    for jh in range(N_IDX_HEADS):
        wb_scr[jh] = jnp.broadcast_to(w[:, jh:jh + 1], (IDX_ROWS, LANES))
    row = t0 + lax.broadcasted_iota(jnp.int32, (IDX_ROWS, LANES), 0)
    lane = lax.broadcasted_iota(jnp.int32, (IDX_ROWS, LANES), 1)

    def score_chunk(c, carry):
        c0 = pl.multiple_of(c * IDX_CH, IDX_CH)
        kc = kit_ref[:, pl.ds(c0, IDX_CH)]
        z = jnp.zeros_like(kc)
        rhs = jnp.concatenate([jnp.concatenate([kc, z], axis=1), jnp.concatenate([z, kc], axis=1)], axis=0)
        accs = [jnp.zeros((IDX_ROWS, LANES), F32) for _ in range(nsub)]
        for p in range(N_IDX_HEADS // 2):
            s2 = jnp.dot(qi_ref[:, p * 2 * IDX_DIM:(p + 1) * 2 * IDX_DIM], rhs, preferred_element_type=F32)
            wa, wb = wb_scr[2 * p], wb_scr[2 * p + 1]
            for u in range(nsub):
                accs[u] = (accs[u] + wa * jnp.maximum(s2[:, u * LANES:(u + 1) * LANES], 0.0)
                           + wb * jnp.maximum(s2[:, IDX_CH + u * LANES:IDX_CH + (u + 1) * LANES], 0.0))
        for u in range(nsub):
            bits = pltpu.bitcast(accs[u], jnp.int32)
            key = bits ^ ((bits >> 31) & 0x7FFFFFFF)
            col = c0 + u * LANES + lane
            keys_scr[:, pl.ds(pl.multiple_of(c0 + u * LANES, LANES), LANES)] = jnp.where(col <= row, key, INT_MIN)
        return carry

    lax.fori_loop(0, nch, score_chunk, 0)

    def count_ge(cand):
        def body(c, acc):
            c0 = pl.multiple_of(c * IDX_CH, IDX_CH)
            blk = keys_scr[:, pl.ds(c0, IDX_CH)]
            for u in range(nsub):
                acc = acc + (blk[:, u * LANES:(u + 1) * LANES] >= cand).astype(jnp.int32)
            return acc
        acc = lax.fori_loop(0, nch, body, jnp.zeros((IDX_ROWS, LANES), jnp.int32))
        return jnp.broadcast_to(jnp.sum(acc, axis=1, keepdims=True), (IDX_ROWS, LANES))

    def lane_all(x, op):
        s = LANES // 2
        while s >= 1:
            x = op(x, pltpu.roll(x, s, axis=1))
            s //= 2
        return x

    def gmax_body(c, gs):
        c0 = pl.multiple_of(c * IDX_CH, IDX_CH)
        blk = keys_scr[:, pl.ds(c0, IDX_CH)]
        return tuple(jnp.maximum(g, blk[:, u * LANES:(u + 1) * LANES]) for u, g in enumerate(gs))

    gs = lax.fori_loop(0, nch, gmax_body, tuple(jnp.full((IDX_ROWS, LANES), INT_MIN, jnp.int32) for _ in range(nsub)))
    gmin, gmax = gs[0], gs[0]
    for g in gs[1:]:
        gmin, gmax = jnp.minimum(gmin, g), jnp.maximum(gmax, g)
    short = row < topk

    def settled(lo, hi, clo):
        done = short | (clo == topk) | (hi - lo == 1)
        return (jnp.min(jnp.where(done, 1.0, 0.0)) > 0.5).astype(jnp.int32)

    def bisect(state):
        it, lo, hi, clo, _ = state
        mid = (lo >> 1) + (hi >> 1) + (lo & hi & 1)
        cnt = count_ge(mid)
        ge = cnt >= topk
        lo, hi, clo = jnp.where(ge, mid, lo), jnp.where(ge, hi, mid), jnp.where(ge, cnt, clo)
        return it + 1, lo, hi, clo, settled(lo, hi, clo)

    lo0 = lane_all(gmin, jnp.minimum)
    hi0 = lane_all(gmax, jnp.maximum) + 1
    clo0 = jnp.full((IDX_ROWS, LANES), -1, jnp.int32)
    _, lo, _, _, _ = lax.while_loop(lambda st: (st[4] == 0) & (st[0] < 34), bisect,
                                    (jnp.int32(0), lo0, hi0, clo0, settled(lo0, hi0, clo0)))
    thr = jnp.where(short, INT_MIN + 1, jnp.maximum(lo, INT_MIN + 1))

    def emit(c, carry):
        c0 = pl.multiple_of(c * IDX_CH, IDX_CH)
        blk = keys_scr[:, pl.ds(c0, IDX_CH)]
        for u in range(nsub):
            sel = blk[:, u * LANES:(u + 1) * LANES] >= thr
            o_ref[:, pl.ds(pl.multiple_of(c0 + u * LANES, LANES), LANES)] = jnp.where(sel, 0.0, NEG_INF).astype(o_ref.dtype)
        return carry

    lax.fori_loop(0, nch, emit, 0)

    def fill(c, carry):
        c0 = pl.multiple_of(c * IDX_CH, IDX_CH)
        o_ref[:, pl.ds(c0, IDX_CH)] = jnp.full((IDX_ROWS, IDX_CH), NEG_INF, o_ref.dtype)
        return carry

    lax.fori_loop(nch, seq // IDX_CH, fill, 0)


def _index(qi, misc3, kit, topk):
    b, seq, w = qi.shape
    return pl.pallas_call(
        functools.partial(_index_kernel, topk=topk),
        out_shape=jax.ShapeDtypeStruct((b, seq, seq), BF16),
        grid=(b, seq // IDX_ROWS),
        in_specs=[pl.BlockSpec((None, IDX_ROWS, w), lambda bb, i: (bb, i, 0)),
                  pl.BlockSpec((None, IDX_ROWS, LANES), lambda bb, i: (bb, i, 0)),
                  pl.BlockSpec((None, IDX_DIM, seq), lambda bb, i: (bb, 0, 0))],
        out_specs=pl.BlockSpec((None, IDX_ROWS, seq), lambda bb, i: (bb, i, 0)),
        scratch_shapes=[pltpu.VMEM((IDX_ROWS, seq), jnp.int32),
                        pltpu.VMEM((N_IDX_HEADS, IDX_ROWS, LANES), F32)],
        compiler_params=_cparams(("parallel", "arbitrary")),
        name="index",
    )(qi, misc3, kit)


def _dsa_kernel(tab_ref, q_ref, k_ref, v_ref, b_ref, o_ref, m_scr, l_scr, acc_ref, toe_scr, s_scr):
    i = pl.program_id(1)
    j = pl.program_id(2)
    tq, tk = q_ref.shape[0], k_ref.shape[0]
    nb = tq // LANES

    @pl.when(j == 0)
    def _():
        _attn_init(m_scr, l_scr, acc_ref)

    @pl.when((i == 0) & (j == 0))
    def _():
        r = lax.broadcasted_iota(jnp.int32, (LANES, LANES), 0)
        c = lax.broadcasted_iota(jnp.int32, (LANES, LANES), 1)
        d_diag, d_sub = r - c, LANES + r - c
        for h in range(N_DSA_HEADS):
            def body(d, carry, h=h):
                td, ts = carry
                val = tab_ref[h, d]
                return jnp.where(d_diag == d, val, td), jnp.where(d_sub == d, val, ts)
            zero = jnp.zeros((LANES, LANES), F32)
            td, ts = lax.fori_loop(0, LANES, body, (zero, zero))
            toe_scr[h, 0] = td
            toe_scr[h, 1] = ts

    def tile(kind):
        bias = b_ref[...].astype(F32)
        for h in range(N_DSA_HEADS):
            s = _qk(q_ref, k_ref, h) + bias
            if kind != "far":
                s_scr[...] = s
                if kind == "diag":
                    for a in range(nb):
                        sl = slice(a * LANES, (a + 1) * LANES)
                        s_scr[sl, sl] += toe_scr[h, 0]
                    for a in range(1, nb):
                        s_scr[a * LANES:(a + 1) * LANES, (a - 1) * LANES:a * LANES] += toe_scr[h, 1]
                else:
                    s_scr[0:LANES, (nb - 1) * LANES:nb * LANES] += toe_scr[h, 1]
                s = s_scr[...]
            _softmax_update(s, v_ref[:, h * HEAD_DIM:(h + 1) * HEAD_DIM], m_scr, l_scr, acc_ref, h)

    @pl.when(j < i - 1)
    def _():
        tile("far")

    @pl.when(j == i - 1)
    def _():
        tile("near")

    @pl.when(j == i)
    def _():
        tile("diag")
        _attn_finish(o_ref, l_scr, acc_ref, N_DSA_HEADS)


def _dsa(tab, q, k, v, bias, t=512):
    b, seq, w = q.shape
    n = seq // t
    kv_map = lambda bb, i, j: (bb, jnp.minimum(j, i), 0)
    return pl.pallas_call(
        _dsa_kernel,
        out_shape=jax.ShapeDtypeStruct((b, seq, w), BF16),
        grid=(b, n, n),
        in_specs=[pl.BlockSpec(memory_space=pltpu.SMEM),
                  pl.BlockSpec((None, t, w), lambda bb, i, j: (bb, i, 0)),
                  pl.BlockSpec((None, t, w), kv_map),
                  pl.BlockSpec((None, t, w), kv_map),
                  pl.BlockSpec((None, t, t), lambda bb, i, j: (bb, i, jnp.minimum(j, i)))],
        out_specs=pl.BlockSpec((None, t, w), lambda bb, i, j: (bb, i, 0)),
        scratch_shapes=[pltpu.VMEM((N_DSA_HEADS, t, LANES), F32),
                        pltpu.VMEM((N_DSA_HEADS, t, LANES), F32),
                        pltpu.VMEM((t, w), F32),
                        pltpu.VMEM((N_DSA_HEADS, 2, LANES, LANES), F32),
                        pltpu.VMEM((t, t), F32)],
        compiler_params=_cparams(("arbitrary", "arbitrary", "arbitrary")),
        name="dsa",
    )(tab, q, k, v, bias)


def _merge_kernel(af_ref, ad_ref, ga_ref, gb_ref, x_ref, mod_ref, wof_ref, wod_ref, wo_ref, g2_ref,
                  x1_ref, h2_ref):
    yf = jnp.dot(af_ref[...], wof_ref[...], preferred_element_type=F32)
    yd = jnp.dot(ad_ref[...], wod_ref[...], preferred_element_type=F32)
    merged = ga_ref[...].astype(F32) * yf + gb_ref[...].astype(F32) * yd
    o = jnp.dot(merged.astype(BF16), wo_ref[...], preferred_element_type=F32)
    x1 = x_ref[...] + mod_ref[2:3, :] * o
    x1_ref[...] = x1
    h2_ref[...] = _modulated_norm(x1, g2_ref[...], mod_ref[4:5, :], mod_ref[3:4, :]).astype(h2_ref.dtype)


def _merge(af, ad, ga, gb, x2, mod3, wof, wod, wo, g2, seq, tm=256):
    m, d = x2.shape
    per_b = seq // tm
    row = lambda i: (i, 0)
    const = lambda i: (0, 0)
    return pl.pallas_call(
        _merge_kernel,
        out_shape=(jax.ShapeDtypeStruct((m, d), F32), jax.ShapeDtypeStruct((m, d), BF16)),
        grid=(m // tm,),
        in_specs=[pl.BlockSpec((tm, FOX_W), row), pl.BlockSpec((tm, DSA_W), row),
                  pl.BlockSpec((tm, d), row), pl.BlockSpec((tm, d), lambda i: (i, 1)), pl.BlockSpec((tm, d), row),
                  pl.BlockSpec((None, 6, d), lambda i: (i // per_b, 0, 0)),
                  pl.BlockSpec((FOX_W, d), const), pl.BlockSpec((DSA_W, d), const), pl.BlockSpec((d, d), const),
                  pl.BlockSpec((1, d), const)],
        out_specs=(pl.BlockSpec((tm, d), row), pl.BlockSpec((tm, d), row)),
        compiler_params=_cparams(("parallel",)),
        name="merge",
    )(af, ad, ga, gb, x2, mod3, wof, wod, wo, g2)


FFN_HALO = 16


def _ffn_kernel(h_ref, halo_ref, wa_ref, wb_ref, cwa_ref, cwb_ref, cba_ref, cbb_ref, wout_ref, x1_ref, mod_ref,
                o_ref, hext_scr, acc_ref, *, per_b):
    i = pl.program_id(0)
    f = pl.program_id(1)
    tm = h_ref.shape[0]

    @pl.when(f == 0)
    def _():
        first = (i % per_b) == 0
        hext_scr[0:FFN_HALO, :] = jnp.where(first, jnp.zeros_like(halo_ref[...]), halo_ref[...])
        hext_scr[FFN_HALO:, :] = h_ref[...]
        acc_ref[...] = jnp.zeros_like(acc_ref)

    hext = hext_scr[...]

    def conv(w_ref, cw_ref, cb_ref):
        u = jnp.dot(hext, w_ref[...], preferred_element_type=F32)
        y = cw_ref[2:3, :] * u + cw_ref[1:2, :] * pltpu.roll(u, 1, axis=0) + cw_ref[0:1, :] * pltpu.roll(u, 2, axis=0)
        return y[FFN_HALO:, :] + cb_ref[...]

    ya = conv(wa_ref, cwa_ref, cba_ref)
    yb = conv(wb_ref, cwb_ref, cbb_ref)
    act = (ya * jax.nn.sigmoid(ya) * yb).astype(BF16)
    acc_ref[...] += jnp.dot(act, wout_ref[...], preferred_element_type=F32)

    @pl.when(f == pl.num_programs(1) - 1)
    def _():
        o_ref[...] = x1_ref[...] + mod_ref[5:6, :] * acc_ref[...]


def _ffn(h2, w_in, conv_w, conv_b, w_out, x1, mod3, seq, tm=512, tf=512):
    m, d = h2.shape
    dff = w_out.shape[0]
    nf = dff // tf
    per_b = seq // tm
    hb = tm // FFN_HALO
    return pl.pallas_call(
        functools.partial(_ffn_kernel, per_b=per_b),
        out_shape=jax.ShapeDtypeStruct((m, d), F32),
        grid=(m // tm, nf),
        in_specs=[pl.BlockSpec((tm, d), lambda i, f: (i, 0)),
                  pl.BlockSpec((FFN_HALO, d), lambda i, f: (jnp.maximum(i * hb - 1, 0), 0)),
                  pl.BlockSpec((d, tf), lambda i, f: (0, f)),
                  pl.BlockSpec((d, tf), lambda i, f: (0, f + nf)),
                  pl.BlockSpec((CONV_WIDTH, tf), lambda i, f: (0, f)),
                  pl.BlockSpec((CONV_WIDTH, tf), lambda i, f: (0, f + nf)),
                  pl.BlockSpec((1, tf), lambda i, f: (0, f)),
                  pl.BlockSpec((1, tf), lambda i, f: (0, f + nf)),
                  pl.BlockSpec((tf, d), lambda i, f: (f, 0)),
                  pl.BlockSpec((tm, d), lambda i, f: (i, 0)),
                  pl.BlockSpec((None, 6, d), lambda i, f: (i // per_b, 0, 0))],
        out_specs=pl.BlockSpec((tm, d), lambda i, f: (i, 0)),
        scratch_shapes=[pltpu.VMEM((tm + FFN_HALO, d), BF16), pltpu.VMEM((tm, d), F32)],
        compiler_params=_cparams(("parallel", "arbitrary")),
        name="ffn",
    )(h2, h2, w_in, w_in, conv_w, conv_w, conv_b, conv_b, w_out, x1, mod3)


def _t5_bucket(n):
    n = jnp.maximum(n, 0)
    max_exact = N_BUCKETS // 2
    nf = jnp.maximum(n, 1).astype(F32)
    large = max_exact + (jnp.log(nf / max_exact) / math.log(MAX_DISTANCE / max_exact)
                         * (N_BUCKETS - max_exact)).astype(jnp.int32)
    large = jnp.minimum(large, N_BUCKETS - 1)
    return jnp.where(n < max_exact, n, large)


def _layer(x, c8, w_ada, b_ada, norm1_g, w_in, b_forget, q_norm_fox, k_norm_fox, kv_norm_g, w_ukv, q_norm_dsa,
           k_norm_dsa, w_out_fox, w_out_dsa, w_out, norm2_g, w_ffn_in, conv_w, conv_b, w_ffn_out, rel_bias):
    b, seq, d = x.shape
    m = b * seq
    topk = min(TOPK_MAX, seq // 4)
    x2 = x.reshape(m, d)

    mod3 = _ada(c8, w_ada, b_ada.reshape(1, -1))[:b].reshape(b, 6, d)
    h1 = _normmod(x2, norm1_g.reshape(1, d), mod3, seq)

    o = 0
    cols = {}
    for name, size in (("qf", FOX_W), ("kf", FOX_W), ("vf", FOX_W), ("fg", N_FOX_HEADS), ("qd", DSA_W),
                       ("ckv", KV_LORA), ("qi", N_IDX_HEADS * IDX_DIM), ("ki", IDX_DIM), ("wi", N_IDX_HEADS),
                       ("ga", d), ("gb", d)):
        cols[name] = w_in[:, o:o + size]
        o += size
    wb = lambda a: a.astype(BF16)
    scale = HEAD_DIM ** -0.5 * LOG2E
    tile_h = lambda g, nh: jnp.tile(g.reshape(1, HEAD_DIM), (1, nh))

    qf = _proj(h1, wb(cols["qf"]), tile_h(q_norm_fox, N_FOX_HEADS) * scale, mode="headnorm", out_dtype=BF16, name="proj_qf")
    kf = _proj(h1, wb(cols["kf"]), tile_h(k_norm_fox, N_FOX_HEADS), mode="headnorm", out_dtype=BF16, name="proj_kf")
    vf = _proj(h1, wb(cols["vf"]), None, mode="plain", out_dtype=BF16, name="proj_vf")
    qd = _proj(h1, wb(cols["qd"]), tile_h(q_norm_dsa, N_DSA_HEADS) * scale, mode="headnorm", out_dtype=BF16, name="proj_qd")
    qi = _proj(h1, wb(cols["qi"]), None, mode="plain", out_dtype=BF16, name="proj_qi")
    gates = _proj(h1, wb(jnp.concatenate([cols["ga"], cols["gb"]], axis=1)), None, mode="sigmoid", out_dtype=BF16,
                  name="proj_gates")
    pad = jnp.zeros((d, LANES - N_FOX_HEADS - N_IDX_HEADS - IDX_DIM), w_in.dtype)
    misc = _proj(h1, wb(jnp.concatenate([cols["fg"], cols["wi"], cols["ki"], pad], axis=1)), None, mode="plain",
                 out_dtype=F32, name="proj_misc")
    kd, vd = _ckv(h1, wb(cols["ckv"]), kv_norm_g.reshape(1, KV_LORA), wb(w_ukv), tile_h(k_norm_dsa, N_DSA_HEADS))

    misc3 = misc.reshape(b, seq, LANES)
    bf = jnp.zeros((1, LANES), F32).at[0, :N_FOX_HEADS].set(b_forget.astype(F32))
    cum = _cumsum(misc3, bf)
    r3 = lambda a: a.reshape(b, seq, -1)
    qa, ka = _augment(qf, kf, cum.reshape(m, LANES))
    a_fox = _fox(r3(qa), r3(ka), r3(vf))

    kit = jnp.swapaxes(misc3[:, :, MISC_KI:MISC_KI + IDX_DIM], 1, 2).astype(BF16)
    sel_bias = _index(r3(qi), misc3, kit, topk)
    by_dist = rel_bias[_t5_bucket(jnp.arange(LANES, dtype=jnp.int32))] - rel_bias[N_BUCKETS - 1][None, :]
    a_dsa = _dsa((by_dist.T * LOG2E).astype(F32), r3(qd), r3(kd), r3(vd), sel_bias)

    x1, h2 = _merge(a_fox.reshape(m, FOX_W), a_dsa.reshape(m, DSA_W), gates, gates, x2, mod3,
                    wb(w_out_fox), wb(w_out_dsa), wb(w_out), norm2_g.reshape(1, d), seq)
    out = _ffn(h2, wb(w_ffn_in), conv_w, conv_b.reshape(1, -1), wb(w_ffn_out), x1, mod3, seq)
    return out.reshape(b, seq, d)


def kernel(x, c, w_ada, b_ada, norm1_g, w_in, b_forget, q_norm_fox, k_norm_fox, kv_norm_g, w_ukv, q_norm_dsa, k_norm_dsa, w_out_fox, w_out_dsa, w_out, norm2_g, w_ffn_in, conv_w, conv_b, w_ffn_out, rel_bias):
    b = x.shape[0]
    c8 = jnp.zeros((8, c.shape[1]), c.dtype).at[:b].set(c)
    for l in range(w_ada.shape[0]):
        x = _layer(x, c8, w_ada[l], b_ada[l], norm1_g[l], w_in[l], b_forget[l], q_norm_fox[l], k_norm_fox[l],
                   kv_norm_g[l], w_ukv[l], q_norm_dsa[l], k_norm_dsa[l], w_out_fox[l], w_out_dsa[l], w_out[l],
                   norm2_g[l], w_ffn_in[l], conv_w[l], conv_b[l], w_ffn_out[l], rel_bias)
    return x
```

```python
import functools
import math

import jax
import jax.numpy as jnp
from jax import lax
from jax.experimental import pallas as pl
from jax.experimental.pallas import tpu as pltpu

HEAD_DIM = 128
N_FOX_HEADS = 8
N_DSA_HEADS = 8
FOX_W = N_FOX_HEADS * HEAD_DIM
DSA_W = N_DSA_HEADS * HEAD_DIM
KV_LORA = 256
N_IDX_HEADS = 16
IDX_DIM = 64
TOPK_MAX = 256
N_BUCKETS = 32
MAX_DISTANCE = 128
CONV_WIDTH = 3
EPS = 1e-6
NEG_INF = -1e30
LOG2E = 1.4426950408889634

LANES = 128
INT_MIN = -(2 ** 31)
VMEM_LIMIT = 56 * 1024 * 1024

F32 = jnp.float32
BF16 = jnp.bfloat16

MISC_FG = 0
MISC_WI = N_FOX_HEADS
MISC_KI = MISC_WI + N_IDX_HEADS


def _cparams(sem):
    return pltpu.CompilerParams(dimension_semantics=sem, vmem_limit_bytes=VMEM_LIMIT)


def _ada_kernel(c_ref, w_ref, b_ref, o_ref):
    c = c_ref[...]
    ca = c * jax.nn.sigmoid(c)
    o_ref[...] = jnp.dot(ca, w_ref[...], preferred_element_type=F32,
                         precision=lax.Precision.HIGHEST) + b_ref[...]


def _ada(c8, w, b, tn=1024):
    rows, d = c8.shape
    n = w.shape[1]
    return pl.pallas_call(
        _ada_kernel,
        out_shape=jax.ShapeDtypeStruct((rows, n), F32),
        grid=(n // tn,),
        in_specs=[pl.BlockSpec((rows, d), lambda j: (0, 0)),
                  pl.BlockSpec((d, tn), lambda j: (0, j)),
                  pl.BlockSpec((1, tn), lambda j: (0, j))],
        out_specs=pl.BlockSpec((rows, tn), lambda j: (0, j)),
        compiler_params=_cparams(("arbitrary",)),
        name="ada",
    )(c8, w, b)


def _modulated_norm(x, g, sc, sh):
    ms = jnp.mean(x * x, axis=-1, keepdims=True)
    return (x * lax.rsqrt(ms + EPS) * g) * (1.0 + sc) + sh


def _normmod_kernel(x_ref, g_ref, mod_ref, o_ref):
    o_ref[...] = _modulated_norm(x_ref[...], g_ref[...], mod_ref[1:2, :], mod_ref[0:1, :]).astype(o_ref.dtype)


def _normmod(x2, g, mod3, seq, tm=512):
    m, d = x2.shape
    per_b = seq // tm
    return pl.pallas_call(
        _normmod_kernel,
        out_shape=jax.ShapeDtypeStruct((m, d), BF16),
        grid=(m // tm,),
        in_specs=[pl.BlockSpec((tm, d), lambda i: (i, 0)),
                  pl.BlockSpec((1, d), lambda i: (0, 0)),
                  pl.BlockSpec((None, 6, d), lambda i: (i // per_b, 0, 0))],
        out_specs=pl.BlockSpec((tm, d), lambda i: (i, 0)),
        compiler_params=_cparams(("parallel",)),
        name="normmod",
    )(x2, g, mod3)


def _head_norm_store(acc, gain_ref, o_ref, col0=0):
    for hh in range(acc.shape[1] // HEAD_DIM):
        a = acc[:, hh * HEAD_DIM:(hh + 1) * HEAD_DIM]
        ms = jnp.mean(a * a, axis=-1, keepdims=True)
        sl = slice(col0 + hh * HEAD_DIM, col0 + (hh + 1) * HEAD_DIM)
        o_ref[:, sl] = (a * lax.rsqrt(ms + EPS) * gain_ref[:, hh * HEAD_DIM:(hh + 1) * HEAD_DIM]).astype(o_ref.dtype)


def _proj_kernel(h_ref, w_ref, gain_ref, o_ref, *, mode):
    acc = jnp.dot(h_ref[...], w_ref[...], preferred_element_type=F32)
    if mode == "headnorm":
        _head_norm_store(acc, gain_ref, o_ref)
    elif mode == "sigmoid":
        o_ref[...] = jax.nn.sigmoid(acc).astype(o_ref.dtype)
    else:
        o_ref[...] = acc.astype(o_ref.dtype)


def _proj(h, w, gain, *, mode, out_dtype, tm=512, tn=1024, name="proj"):
    m, d = h.shape
    n = w.shape[1]
    tn = min(tn, n)
    if gain is None:
        gain = jnp.ones((1, n), F32)
    return pl.pallas_call(
        functools.partial(_proj_kernel, mode=mode),
        out_shape=jax.ShapeDtypeStruct((m, n), out_dtype),
        grid=(n // tn, m // tm),
        in_specs=[pl.BlockSpec((tm, d), lambda j, i: (i, 0)),
                  pl.BlockSpec((d, tn), lambda j, i: (0, j)),
                  pl.BlockSpec((1, tn), lambda j, i: (0, j))],
        out_specs=pl.BlockSpec((tm, tn), lambda j, i: (i, j)),
        compiler_params=_cparams(("parallel", "parallel")),
        name=name,
    )(h, w, gain)


def _ckv_kernel(h_ref, wc_ref, g_ref, wu_ref, gk_ref, k_ref, v_ref):
    c = jnp.dot(h_ref[...], wc_ref[...], preferred_element_type=F32)
    ms = jnp.mean(c * c, axis=-1, keepdims=True)
    cn = (c * lax.rsqrt(ms + EPS) * g_ref[...]).astype(BF16)
    kv = jnp.dot(cn, wu_ref[...], preferred_element_type=F32)
    _head_norm_store(kv[:, :DSA_W], gk_ref, k_ref)
    v_ref[...] = kv[:, DSA_W:].astype(v_ref.dtype)


def _ckv(h, wc, g, wu, gk, tm=512):
    m, d = h.shape
    return pl.pallas_call(
        _ckv_kernel,
        out_shape=(jax.ShapeDtypeStruct((m, DSA_W), BF16), jax.ShapeDtypeStruct((m, DSA_W), BF16)),
        grid=(m // tm,),
        in_specs=[pl.BlockSpec((tm, d), lambda i: (i, 0)),
                  pl.BlockSpec((d, KV_LORA), lambda i: (0, 0)),
                  pl.BlockSpec((1, KV_LORA), lambda i: (0, 0)),
                  pl.BlockSpec((KV_LORA, 2 * DSA_W), lambda i: (0, 0)),
                  pl.BlockSpec((1, DSA_W), lambda i: (0, 0))],
        out_specs=(pl.BlockSpec((tm, DSA_W), lambda i: (i, 0)), pl.BlockSpec((tm, DSA_W), lambda i: (i, 0))),
        compiler_params=_cparams(("parallel",)),
        name="ckv",
    )(h, wc, g, wu, gk)


def _cum_kernel(m_ref, bf_ref, o_ref, carry_ref):
    @pl.when(pl.program_id(1) == 0)
    def _():
        carry_ref[...] = jnp.zeros_like(carry_ref)

    z = m_ref[...] + bf_ref[...]
    lf = -(jnp.maximum(-z, 0.0) + jnp.log1p(jnp.exp(-jnp.abs(z))))
    tc = lf.shape[0]
    row = lax.broadcasted_iota(jnp.int32, lf.shape, 0)
    s = 1
    while s < tc:
        lf = lf + jnp.where(row >= s, pltpu.roll(lf, s, axis=0), 0.0)
        s *= 2
    out = lf + carry_ref[0:1, :]
    o_ref[...] = out
    carry_ref[...] = jnp.broadcast_to(out[tc - 1:tc, :], carry_ref.shape)


def _cumsum(misc3, bf, tc=512):
    b, seq, w = misc3.shape
    return pl.pallas_call(
        _cum_kernel,
        out_shape=jax.ShapeDtypeStruct((b, seq, w), F32),
        grid=(b, seq // tc),
        in_specs=[pl.BlockSpec((None, tc, w), lambda bb, i: (bb, i, 0)),
                  pl.BlockSpec((1, w), lambda bb, i: (0, 0))],
        out_specs=pl.BlockSpec((None, tc, w), lambda bb, i: (bb, i, 0)),
        scratch_shapes=[pltpu.VMEM((8, w), F32)],
        compiler_params=_cparams(("arbitrary", "arbitrary")),
        name="cumsum",
    )(misc3, bf)


def _softmax_update(s, v_h, m_scr, l_scr, acc_ref, h):
    m_prev = m_scr[h]
    m_new = jnp.maximum(m_prev, jnp.max(s, axis=1, keepdims=True))
    alpha = jnp.exp2(m_prev - m_new)
    p = jnp.exp2(s - m_new[:, :1]).astype(BF16)
    v_ones = jnp.concatenate([v_h, jnp.ones_like(v_h)], axis=1)
    pv = jnp.dot(p, v_ones, preferred_element_type=F32)
    l_scr[h] = alpha * l_scr[h] + pv[:, HEAD_DIM:]
    m_scr[h] = m_new
    sl = slice(h * HEAD_DIM, (h + 1) * HEAD_DIM)
    acc_ref[:, sl] = acc_ref[:, sl] * alpha + pv[:, :HEAD_DIM]


def _attn_init(m_scr, l_scr, acc_ref):
    m_scr[...] = jnp.full(m_scr.shape, NEG_INF, F32)
    l_scr[...] = jnp.zeros(l_scr.shape, F32)
    acc_ref[...] = jnp.zeros(acc_ref.shape, F32)


def _attn_finish(o_ref, l_scr, acc_ref, nheads):
    for h in range(nheads):
        sl = slice(h * HEAD_DIM, (h + 1) * HEAD_DIM)
        o_ref[:, sl] = (acc_ref[:, sl] / l_scr[h]).astype(o_ref.dtype)


def _qk(q_ref, k_ref, h, width=HEAD_DIM):
    sl = slice(h * width, (h + 1) * width)
    return lax.dot_general(q_ref[:, sl], k_ref[:, sl], (((1,), (1,)), ((), ())), preferred_element_type=F32)


AUG_W = 2 * HEAD_DIM
N_PIECES = 3


def _aug_kernel(q_ref, k_ref, cum_ref, selq_ref, selk_ref, cq_ref, ck_ref, qa_ref, ka_ref):
    lane = lax.broadcasted_iota(jnp.int32, cum_ref.shape, 1)
    c = jnp.where(lane < N_FOX_HEADS, cum_ref[...] * LOG2E, 0.0)
    hi = c.astype(BF16).astype(F32)
    r1 = c - hi
    mid = r1.astype(BF16).astype(F32)
    lo = (r1 - mid).astype(BF16).astype(F32)
    pieces = (hi + pltpu.roll(mid, N_FOX_HEADS, axis=1) + pltpu.roll(lo, 2 * N_FOX_HEADS, axis=1)).astype(BF16)
    for h in range(N_FOX_HEADS):
        sl = slice(h * HEAD_DIM, (h + 1) * HEAD_DIM)
        qa_ref[:, h * AUG_W:h * AUG_W + HEAD_DIM] = q_ref[:, sl]
        ka_ref[:, h * AUG_W:h * AUG_W + HEAD_DIM] = k_ref[:, sl]
        eq = jnp.dot(pieces, selq_ref[h], preferred_element_type=F32) + cq_ref[...]
        ek = jnp.dot(pieces, selk_ref[h], preferred_element_type=F32) + ck_ref[...]
        qa_ref[:, h * AUG_W + HEAD_DIM:(h + 1) * AUG_W] = eq.astype(BF16)
        ka_ref[:, h * AUG_W + HEAD_DIM:(h + 1) * AUG_W] = ek.astype(BF16)


def _aug_tables():
    selq = [[[0.0] * LANES for _ in range(LANES)] for _ in range(N_FOX_HEADS)]
    selk = [[[0.0] * LANES for _ in range(LANES)] for _ in range(N_FOX_HEADS)]
    for h in range(N_FOX_HEADS):
        for p in range(N_PIECES):
            selq[h][p * N_FOX_HEADS + h][p] = 1.0
            selk[h][p * N_FOX_HEADS + h][N_PIECES + p] = -1.0
    cq = [[1.0 if N_PIECES <= c < 2 * N_PIECES else 0.0 for c in range(LANES)]]
    ck = [[1.0 if c < N_PIECES else 0.0 for c in range(LANES)]]
    return (jnp.array(selq, BF16), jnp.array(selk, BF16), jnp.array(cq, F32), jnp.array(ck, F32))


def _augment(q, k, cum2, tm=512):
    m, w = q.shape
    selq, selk, cq, ck = _aug_tables()
    row = lambda i: (i, 0)
    const3 = lambda i: (0, 0, 0)
    const2 = lambda i: (0, 0)
    wa = N_FOX_HEADS * AUG_W
    return pl.pallas_call(
        _aug_kernel,
        out_shape=(jax.ShapeDtypeStruct((m, wa), BF16), jax.ShapeDtypeStruct((m, wa), BF16)),
        grid=(m // tm,),
        in_specs=[pl.BlockSpec((tm, w), row), pl.BlockSpec((tm, w), row), pl.BlockSpec((tm, LANES), row),
                  pl.BlockSpec((N_FOX_HEADS, LANES, LANES), const3), pl.BlockSpec((N_FOX_HEADS, LANES, LANES), const3),
                  pl.BlockSpec((1, LANES), const2), pl.BlockSpec((1, LANES), const2)],
        out_specs=(pl.BlockSpec((tm, wa), row), pl.BlockSpec((tm, wa), row)),
        compiler_params=_cparams(("parallel",)),
        name="augment",
    )(q, k, cum2, selq, selk, cq, ck)


ATT_TQ = 1024
ATT_TK = 512


def _causal_pairs(seq, tq, tk):
    r = tq // tk
    pairs = [(i, j) for i in range(seq // tq) for j in range(r * (i + 1))]
    return (jnp.array([p[0] for p in pairs], jnp.int32), jnp.array([p[1] for p in pairs], jnp.int32))


def _fox_kernel(qi_ref, kj_ref, q_ref, k_ref, v_ref, o_ref, m_scr, l_scr, acc_ref):
    t = pl.program_id(1)
    i, j = qi_ref[t], kj_ref[t]
    tq, tk = q_ref.shape[0], k_ref.shape[0]
    r = tq // tk

    @pl.when(j == 0)
    def _():
        _attn_init(m_scr, l_scr, acc_ref)

    def tile(masked):
        if masked:
            col = lax.broadcasted_iota(jnp.int32, (tq, tk), 1) + (j * tk - i * tq)
            keep = col <= lax.broadcasted_iota(jnp.int32, (tq, tk), 0)
        for h in range(N_FOX_HEADS):
            s = _qk(q_ref, k_ref, h, AUG_W)
            if masked:
                s = jnp.where(keep, s, NEG_INF)
            _softmax_update(s, v_ref[:, h * HEAD_DIM:(h + 1) * HEAD_DIM], m_scr, l_scr, acc_ref, h)

    @pl.when(j < r * i)
    def _():
        tile(False)

    @pl.when(j >= r * i)
    def _():
        tile(True)

    @pl.when(j == r * i + r - 1)
    def _():
        _attn_finish(o_ref, l_scr, acc_ref, N_FOX_HEADS)


def _fox(qa, ka, v):
    b, seq, w = v.shape
    wa = qa.shape[2]
    tq, tk = min(ATT_TQ, seq), min(ATT_TK, seq)
    qi, kj = _causal_pairs(seq, tq, tk)
    q_map = lambda bb, t, qi, kj: (bb, qi[t], 0)
    kv_map = lambda bb, t, qi, kj: (bb, kj[t], 0)
    return pl.pallas_call(
        _fox_kernel,
        out_shape=jax.ShapeDtypeStruct((b, seq, w), BF16),
        grid_spec=pltpu.PrefetchScalarGridSpec(
            num_scalar_prefetch=2,
            grid=(b, qi.shape[0]),
            in_specs=[pl.BlockSpec((None, tq, wa), q_map),
                      pl.BlockSpec((None, tk, wa), kv_map),
                      pl.BlockSpec((None, tk, w), kv_map)],
            out_specs=pl.BlockSpec((None, tq, w), q_map),
            scratch_shapes=[pltpu.VMEM((N_FOX_HEADS, tq, LANES), F32),
                            pltpu.VMEM((N_FOX_HEADS, tq, LANES), F32),
                            pltpu.VMEM((tq, w), F32)]),
        compiler_params=_cparams(("parallel", "arbitrary")),
        name="fox",
    )(qi, kj, qa, ka, v)


IDX_ROWS = 128
IDX_CH = 256


def _index_kernel(qi_ref, misc_ref, kit_ref, o_ref, keys_scr, wb_scr, *, topk):
    i = pl.program_id(1)
    seq = kit_ref.shape[1]
    nsub = IDX_CH // LANES
    t0 = i * IDX_ROWS
    nch = (t0 + IDX_ROWS + IDX_CH - 1) // IDX_CH
    w = misc_ref[:, MISC_WI:MISC_WI + N_IDX_HEADS] * (N_IDX_HEADS ** -0.5 * IDX_DIM ** -0.5)
    for jh in range(N_IDX_HEADS):
        wb_scr[jh] = jnp.broadcast_to(w[:, jh:jh + 1], (IDX_ROWS, LANES))
    row = t0 + lax.broadcasted_iota(jnp.int32, (IDX_ROWS, LANES), 0)
    lane = lax.broadcasted_iota(jnp.int32, (IDX_ROWS, LANES), 1)

    def score_chunk(c, carry):
        c0 = pl.multiple_of(c * IDX_CH, IDX_CH)
        kc = kit_ref[:, pl.ds(c0, IDX_CH)]
        z = jnp.zeros_like(kc)
        rhs = jnp.concatenate([jnp.concatenate([kc, z], axis=1), jnp.concatenate([z, kc], axis=1)], axis=0)
        accs = [jnp.zeros((IDX_ROWS, LANES), F32) for _ in range(nsub)]
        for p in range(N_IDX_HEADS // 2):
            s2 = jnp.dot(qi_ref[:, p * 2 * IDX_DIM:(p + 1) * 2 * IDX_DIM], rhs, preferred_element_type=F32)
            wa, wb = wb_scr[2 * p], wb_scr[2 * p + 1]
            for u in range(nsub):
                accs[u] = (accs[u] + wa * jnp.maximum(s2[:, u * LANES:(u + 1) * LANES], 0.0)
                           + wb * jnp.maximum(s2[:, IDX_CH + u * LANES:IDX_CH + (u + 1) * LANES], 0.0))
        for u in range(nsub):
            bits = pltpu.bitcast(accs[u], jnp.int32)
            key = bits ^ ((bits >> 31) & 0x7FFFFFFF)
            col = c0 + u * LANES + lane
            keys_scr[:, pl.ds(pl.multiple_of(c0 + u * LANES, LANES), LANES)] = jnp.where(col <= row, key, INT_MIN)
        return carry

    def score_pair(cp, carry):
        score_chunk(2 * cp, carry)
        return score_chunk(2 * cp + 1, carry)

    lax.fori_loop(0, (nch + 1) // 2, score_pair, 0)

    def count_ge(cand):
        def body(c, acc):
            c0 = pl.multiple_of(c * IDX_CH, IDX_CH)
            blk = keys_scr[:, pl.ds(c0, IDX_CH)]
            for u in range(nsub):
                acc = acc + jnp.where(blk[:, u * LANES:(u + 1) * LANES] >= cand, 1.0, 0.0)
            return acc
        acc = lax.fori_loop(0, nch, body, jnp.zeros((IDX_ROWS, LANES), F32))
        return jnp.broadcast_to(jnp.sum(acc, axis=1, keepdims=True), (IDX_ROWS, LANES))

    def lane_all(x, op):
        s = LANES // 2
        while s >= 1:
            x = op(x, pltpu.roll(x, s, axis=1))
            s //= 2
        return x

    def gmax_body(c, gs):
        c0 = pl.multiple_of(c * IDX_CH, IDX_CH)
        blk = keys_scr[:, pl.ds(c0, IDX_CH)]
        return tuple(jnp.maximum(g, blk[:, u * LANES:(u + 1) * LANES]) for u, g in enumerate(gs))

    gs = lax.fori_loop(0, nch, gmax_body, tuple(jnp.full((IDX_ROWS, LANES), INT_MIN, jnp.int32) for _ in range(nsub)))
    gmin, gmax = gs[0], gs[0]
    for g in gs[1:]:
        gmin, gmax = jnp.minimum(gmin, g), jnp.maximum(gmax, g)
    short = row < topk

    def settled(lo, hi, clo):
        done = short | (clo == topk) | (hi - lo == 1)
        return (jnp.min(jnp.where(done, 1.0, 0.0)) > 0.5).astype(jnp.int32)

    def bisect(state):
        it, lo, hi, clo, _ = state
        mid = (lo >> 1) + (hi >> 1) + (lo & hi & 1)
        cnt = count_ge(mid)
        ge = cnt >= topk
        lo, hi, clo = jnp.where(ge, mid, lo), jnp.where(ge, hi, mid), jnp.where(ge, cnt, clo)
        return it + 1, lo, hi, clo, settled(lo, hi, clo)

    lo0 = lane_all(gmin, jnp.minimum)
    hi0 = lane_all(gmax, jnp.maximum) + 1
    clo0 = jnp.full((IDX_ROWS, LANES), -1.0, F32)
    _, lo, _, _, _ = lax.while_loop(lambda st: (st[4] == 0) & (st[0] < 34), bisect,
                                    (jnp.int32(0), lo0, hi0, clo0, settled(lo0, hi0, clo0)))
    thr = jnp.where(short, INT_MIN + 1, jnp.maximum(lo, INT_MIN + 1))

    def emit(c, carry):
        c0 = pl.multiple_of(c * IDX_CH, IDX_CH)
        blk = keys_scr[:, pl.ds(c0, IDX_CH)]
        for u in range(nsub):
            sel = blk[:, u * LANES:(u + 1) * LANES] >= thr
            o_ref[:, pl.ds(pl.multiple_of(c0 + u * LANES, LANES), LANES)] = jnp.where(sel, 0.0, NEG_INF).astype(o_ref.dtype)
        return carry

    lax.fori_loop(0, nch, emit, 0)

    def fill(c, carry):
        c0 = pl.multiple_of(c * IDX_CH, IDX_CH)
        o_ref[:, pl.ds(c0, IDX_CH)] = jnp.full((IDX_ROWS, IDX_CH), NEG_INF, o_ref.dtype)
        return carry

    lax.fori_loop(nch, seq // IDX_CH, fill, 0)


def _index(qi, misc3, kit, topk):
    b, seq, w = qi.shape
    return pl.pallas_call(
        functools.partial(_index_kernel, topk=topk),
        out_shape=jax.ShapeDtypeStruct((b, seq, seq), BF16),
        grid=(b, seq // IDX_ROWS),
        in_specs=[pl.BlockSpec((None, IDX_ROWS, w), lambda bb, i: (bb, i, 0)),
                  pl.BlockSpec((None, IDX_ROWS, LANES), lambda bb, i: (bb, i, 0)),
                  pl.BlockSpec((None, IDX_DIM, seq), lambda bb, i: (bb, 0, 0))],
        out_specs=pl.BlockSpec((None, IDX_ROWS, seq), lambda bb, i: (bb, i, 0)),
        scratch_shapes=[pltpu.VMEM((IDX_ROWS, seq), jnp.int32),
                        pltpu.VMEM((N_IDX_HEADS, IDX_ROWS, LANES), F32)],
        compiler_params=_cparams(("parallel", "arbitrary")),
        name="index",
    )(qi, misc3, kit)


def _dsa_kernel(qi_ref, kj_ref, tab_ref, q_ref, k_ref, v_ref, b_ref, o_ref, m_scr, l_scr, acc_ref, toe_scr, s_scr):
    t = pl.program_id(1)
    i, j = qi_ref[t], kj_ref[t]
    tq, tk = q_ref.shape[0], k_ref.shape[0]
    r = tq // tk
    nbq, nbk = tq // LANES, tk // LANES

    @pl.when(j == 0)
    def _():
        _attn_init(m_scr, l_scr, acc_ref)

    @pl.when(t == 0)
    def _():
        row = lax.broadcasted_iota(jnp.int32, (LANES, LANES), 0)
        col = lax.broadcasted_iota(jnp.int32, (LANES, LANES), 1)
        d_diag, d_sub = row - col, LANES + row - col
        for h in range(N_DSA_HEADS):
            def body(d, carry, h=h):
                td, ts = carry
                val = tab_ref[h, d]
                return jnp.where(d_diag == d, val, td), jnp.where(d_sub == d, val, ts)
            zero = jnp.zeros((LANES, LANES), F32)
            td, ts = lax.fori_loop(0, LANES, body, (zero, zero))
            toe_scr[h, 0] = td
            toe_scr[h, 1] = ts

    def tile(e):
        near = [] if e is None else [(a, c, a - c - e * nbk) for a in range(nbq) for c in range(nbk)
                                     if a - c - e * nbk in (0, 1)]
        bias = b_ref[...].astype(F32)
        for h in range(N_DSA_HEADS):
            s = _qk(q_ref, k_ref, h) + bias
            if near:
                s_scr[...] = s
                for a, c, which in near:
                    s_scr[a * LANES:(a + 1) * LANES, c * LANES:(c + 1) * LANES] += toe_scr[h, which]
                s = s_scr[...]
            _softmax_update(s, v_ref[:, h * HEAD_DIM:(h + 1) * HEAD_DIM], m_scr, l_scr, acc_ref, h)

    @pl.when(j < r * i - 1)
    def _():
        tile(None)

    for e in range(-1, r):
        @pl.when(j == r * i + e)
        def _(e=e):
            tile(e)

    @pl.when(j == r * i + r - 1)
    def _():
        _attn_finish(o_ref, l_scr, acc_ref, N_DSA_HEADS)


def _dsa(tab, q, k, v, bias):
    b, seq, w = q.shape
    tq, tk = min(ATT_TQ, seq), min(ATT_TK, seq)
    qi, kj = _causal_pairs(seq, tq, tk)
    q_map = lambda bb, t, qi, kj: (bb, qi[t], 0)
    kv_map = lambda bb, t, qi, kj: (bb, kj[t], 0)
    return pl.pallas_call(
        _dsa_kernel,
        out_shape=jax.ShapeDtypeStruct((b, seq, w), BF16),
        grid_spec=pltpu.PrefetchScalarGridSpec(
            num_scalar_prefetch=2,
            grid=(b, qi.shape[0]),
            in_specs=[pl.BlockSpec(memory_space=pltpu.SMEM),
                      pl.BlockSpec((None, tq, w), q_map),
                      pl.BlockSpec((None, tk, w), kv_map),
                      pl.BlockSpec((None, tk, w), kv_map),
                      pl.BlockSpec((None, tq, tk), lambda bb, t, qi, kj: (bb, qi[t], kj[t]))],
            out_specs=pl.BlockSpec((None, tq, w), q_map),
            scratch_shapes=[pltpu.VMEM((N_DSA_HEADS, tq, LANES), F32),
                            pltpu.VMEM((N_DSA_HEADS, tq, LANES), F32),
                            pltpu.VMEM((tq, w), F32),
                            pltpu.VMEM((N_DSA_HEADS, 2, LANES, LANES), F32),
                            pltpu.VMEM((tq, tk), F32)]),
        compiler_params=_cparams(("arbitrary", "arbitrary")),
        name="dsa",
    )(qi, kj, tab, q, k, v, bias)


def _merge_kernel(af_ref, ad_ref, ga_ref, gb_ref, x_ref, mod_ref, wof_ref, wod_ref, wo_ref, g2_ref,
                  x1_ref, h2_ref):
    yf = jnp.dot(af_ref[...], wof_ref[...], preferred_element_type=F32)
    yd = jnp.dot(ad_ref[...], wod_ref[...], preferred_element_type=F32)
    merged = ga_ref[...].astype(F32) * yf + gb_ref[...].astype(F32) * yd
    o = jnp.dot(merged.astype(BF16), wo_ref[...], preferred_element_type=F32)
    x1 = x_ref[...] + mod_ref[2:3, :] * o
    x1_ref[...] = x1
    h2_ref[...] = _modulated_norm(x1, g2_ref[...], mod_ref[4:5, :], mod_ref[3:4, :]).astype(h2_ref.dtype)


def _merge(af, ad, ga, gb, x2, mod3, wof, wod, wo, g2, seq, tm=256):
    m, d = x2.shape
    per_b = seq // tm
    row = lambda i: (i, 0)
    const = lambda i: (0, 0)
    return pl.pallas_call(
        _merge_kernel,
        out_shape=(jax.ShapeDtypeStruct((m, d), F32), jax.ShapeDtypeStruct((m, d), BF16)),
        grid=(m // tm,),
        in_specs=[pl.BlockSpec((tm, FOX_W), row), pl.BlockSpec((tm, DSA_W), row),
                  pl.BlockSpec((tm, d), row), pl.BlockSpec((tm, d), lambda i: (i, 1)), pl.BlockSpec((tm, d), row),
                  pl.BlockSpec((None, 6, d), lambda i: (i // per_b, 0, 0)),
                  pl.BlockSpec((FOX_W, d), const), pl.BlockSpec((DSA_W, d), const), pl.BlockSpec((d, d), const),
                  pl.BlockSpec((1, d), const)],
        out_specs=(pl.BlockSpec((tm, d), row), pl.BlockSpec((tm, d), row)),
        compiler_params=_cparams(("parallel",)),
        name="merge",
    )(af, ad, ga, gb, x2, mod3, wof, wod, wo, g2)


FFN_HALO = 16


def _ffn_kernel(h_ref, halo_ref, wa_ref, wb_ref, cwa_ref, cwb_ref, cba_ref, cbb_ref, wout_ref, x1_ref, mod_ref,
                o_ref, hext_scr, acc_ref, *, per_b):
    i = pl.program_id(0)
    f = pl.program_id(1)
    tm = h_ref.shape[0]

    @pl.when(f == 0)
    def _():
        first = (i % per_b) == 0
        hext_scr[0:FFN_HALO, :] = jnp.where(first, jnp.zeros_like(halo_ref[...]), halo_ref[...])
        hext_scr[FFN_HALO:, :] = h_ref[...]
        acc_ref[...] = jnp.zeros_like(acc_ref)

    hext = hext_scr[...]

    def conv(w_ref, cw_ref, cb_ref):
        u = jnp.dot(hext, w_ref[...], preferred_element_type=F32)
        y = cw_ref[2:3, :] * u + cw_ref[1:2, :] * pltpu.roll(u, 1, axis=0) + cw_ref[0:1, :] * pltpu.roll(u, 2, axis=0)
        return y[FFN_HALO:, :] + cb_ref[...]

    ya = conv(wa_ref, cwa_ref, cba_ref)
    yb = conv(wb_ref, cwb_ref, cbb_ref)
    act = (ya * jax.nn.sigmoid(ya) * yb).astype(BF16)
    acc_ref[...] += jnp.dot(act, wout_ref[...], preferred_element_type=F32)

    @pl.when(f == pl.num_programs(1) - 1)
    def _():
        o_ref[...] = x1_ref[...] + mod_ref[5:6, :] * acc_ref[...]


def _ffn(h2, w_in, conv_w, conv_b, w_out, x1, mod3, seq, tm=512, tf=512):
    m, d = h2.shape
    dff = w_out.shape[0]
    nf = dff // tf
    per_b = seq // tm
    hb = tm // FFN_HALO
    return pl.pallas_call(
        functools.partial(_ffn_kernel, per_b=per_b),
        out_shape=jax.ShapeDtypeStruct((m, d), F32),
        grid=(m // tm, nf),
        in_specs=[pl.BlockSpec((tm, d), lambda i, f: (i, 0)),
                  pl.BlockSpec((FFN_HALO, d), lambda i, f: (jnp.maximum(i * hb - 1, 0), 0)),
                  pl.BlockSpec((d, tf), lambda i, f: (0, f)),
                  pl.BlockSpec((d, tf), lambda i, f: (0, f + nf)),
                  pl.BlockSpec((CONV_WIDTH, tf), lambda i, f: (0, f)),
                  pl.BlockSpec((CONV_WIDTH, tf), lambda i, f: (0, f + nf)),
                  pl.BlockSpec((1, tf), lambda i, f: (0, f)),
                  pl.BlockSpec((1, tf), lambda i, f: (0, f + nf)),
                  pl.BlockSpec((tf, d), lambda i, f: (f, 0)),
                  pl.BlockSpec((tm, d), lambda i, f: (i, 0)),
                  pl.BlockSpec((None, 6, d), lambda i, f: (i // per_b, 0, 0))],
        out_specs=pl.BlockSpec((tm, d), lambda i, f: (i, 0)),
        scratch_shapes=[pltpu.VMEM((tm + FFN_HALO, d), BF16), pltpu.VMEM((tm, d), F32)],
        compiler_params=_cparams(("parallel", "arbitrary")),
        name="ffn",
    )(h2, h2, w_in, w_in, conv_w, conv_w, conv_b, conv_b, w_out, x1, mod3)


def _t5_bucket(n):
    n = jnp.maximum(n, 0)
    max_exact = N_BUCKETS // 2
    nf = jnp.maximum(n, 1).astype(F32)
    large = max_exact + (jnp.log(nf / max_exact) / math.log(MAX_DISTANCE / max_exact)
                         * (N_BUCKETS - max_exact)).astype(jnp.int32)
    large = jnp.minimum(large, N_BUCKETS - 1)
    return jnp.where(n < max_exact, n, large)


def _layer(x, c8, w_ada, b_ada, norm1_g, w_in, b_forget, q_norm_fox, k_norm_fox, kv_norm_g, w_ukv, q_norm_dsa,
           k_norm_dsa, w_out_fox, w_out_dsa, w_out, norm2_g, w_ffn_in, conv_w, conv_b, w_ffn_out, rel_bias):
    b, seq, d = x.shape
    m = b * seq
    topk = min(TOPK_MAX, seq // 4)
    x2 = x.reshape(m, d)

    mod3 = _ada(c8, w_ada, b_ada.reshape(1, -1))[:b].reshape(b, 6, d)
    h1 = _normmod(x2, norm1_g.reshape(1, d), mod3, seq)

    o = 0
    cols = {}
    for name, size in (("qf", FOX_W), ("kf", FOX_W), ("vf", FOX_W), ("fg", N_FOX_HEADS), ("qd", DSA_W),
                       ("ckv", KV_LORA), ("qi", N_IDX_HEADS * IDX_DIM), ("ki", IDX_DIM), ("wi", N_IDX_HEADS),
                       ("ga", d), ("gb", d)):
        cols[name] = w_in[:, o:o + size]
        o += size
    wb = lambda a: a.astype(BF16)
    scale = HEAD_DIM ** -0.5 * LOG2E
    tile_h = lambda g, nh: jnp.tile(g.reshape(1, HEAD_DIM), (1, nh))

    qf = _proj(h1, wb(cols["qf"]), tile_h(q_norm_fox, N_FOX_HEADS) * scale, mode="headnorm", out_dtype=BF16, name="proj_qf")
    kf = _proj(h1, wb(cols["kf"]), tile_h(k_norm_fox, N_FOX_HEADS), mode="headnorm", out_dtype=BF16, name="proj_kf")
    vf = _proj(h1, wb(cols["vf"]), None, mode="plain", out_dtype=BF16, name="proj_vf")
    qd = _proj(h1, wb(cols["qd"]), tile_h(q_norm_dsa, N_DSA_HEADS) * scale, mode="headnorm", out_dtype=BF16, name="proj_qd")
    qi = _proj(h1, wb(cols["qi"]), None, mode="plain", out_dtype=BF16, name="proj_qi")
    gates = _proj(h1, wb(jnp.concatenate([cols["ga"], cols["gb"]], axis=1)), None, mode="sigmoid", out_dtype=BF16,
                  name="proj_gates")
    pad = jnp.zeros((d, LANES - N_FOX_HEADS - N_IDX_HEADS - IDX_DIM), w_in.dtype)
    misc = _proj(h1, wb(jnp.concatenate([cols["fg"], cols["wi"], cols["ki"], pad], axis=1)), None, mode="plain",
                 out_dtype=F32, name="proj_misc")
    kd, vd = _ckv(h1, wb(cols["ckv"]), kv_norm_g.reshape(1, KV_LORA), wb(w_ukv), tile_h(k_norm_dsa, N_DSA_HEADS))

    misc3 = misc.reshape(b, seq, LANES)
    bf = jnp.zeros((1, LANES), F32).at[0, :N_FOX_HEADS].set(b_forget.astype(F32))
    cum = _cumsum(misc3, bf)
    r3 = lambda a: a.reshape(b, seq, -1)
    qa, ka = _augment(qf, kf, cum.reshape(m, LANES))
    a_fox = _fox(r3(qa), r3(ka), r3(vf))

    kit = jnp.swapaxes(misc3[:, :, MISC_KI:MISC_KI + IDX_DIM], 1, 2).astype(BF16)
    sel_bias = _index(r3(qi), misc3, kit, topk)
    by_dist = rel_bias[_t5_bucket(jnp.arange(LANES, dtype=jnp.int32))] - rel_bias[N_BUCKETS - 1][None, :]
    a_dsa = _dsa((by_dist.T * LOG2E).astype(F32), r3(qd), r3(kd), r3(vd), sel_bias)

    x1, h2 = _merge(a_fox.reshape(m, FOX_W), a_dsa.reshape(m, DSA_W), gates, gates, x2, mod3,
                    wb(w_out_fox), wb(w_out_dsa), wb(w_out), norm2_g.reshape(1, d), seq)
    out = _ffn(h2, wb(w_ffn_in), conv_w, conv_b.reshape(1, -1), wb(w_ffn_out), x1, mod3, seq)
    return out.reshape(b, seq, d)


def kernel(x, c, w_ada, b_ada, norm1_g, w_in, b_forget, q_norm_fox, k_norm_fox, kv_norm_g, w_ukv, q_norm_dsa, k_norm_dsa, w_out_fox, w_out_dsa, w_out, norm2_g, w_ffn_in, conv_w, conv_b, w_ffn_out, rel_bias):
    b = x.shape[0]
    c8 = jnp.zeros((8, c.shape[1]), c.dtype).at[:b].set(c)
    for l in range(w_ada.shape[0]):
        x = _layer(x, c8, w_ada[l], b_ada[l], norm1_g[l], w_in[l], b_forget[l], q_norm_fox[l], k_norm_fox[l],
                   kv_norm_g[l], w_ukv[l], q_norm_dsa[l], k_norm_dsa[l], w_out_fox[l], w_out_dsa[l], w_out[l],
                   norm2_g[l], w_ffn_in[l], conv_w[l], conv_b[l], w_ffn_out[l], rel_bias)
    return x
```

```python
import functools
import math

import jax
import jax.numpy as jnp
from jax import lax
from jax.experimental import pallas as pl
from jax.experimental.pallas import tpu as pltpu

HEAD_DIM = 128
N_FOX_HEADS = 8
N_DSA_HEADS = 8
FOX_W = N_FOX_HEADS * HEAD_DIM
DSA_W = N_DSA_HEADS * HEAD_DIM
KV_LORA = 256
N_IDX_HEADS = 16
IDX_DIM = 64
TOPK_MAX = 256
N_BUCKETS = 32
MAX_DISTANCE = 128
CONV_WIDTH = 3
EPS = 1e-6
NEG_INF = -1e30
LOG2E = 1.4426950408889634

LANES = 128
INT_MIN = -(2 ** 31)
VMEM_LIMIT = 56 * 1024 * 1024

F32 = jnp.float32
BF16 = jnp.bfloat16

MISC_FG = 0
MISC_WI = N_FOX_HEADS
MISC_KI = MISC_WI + N_IDX_HEADS


def _cparams(sem):
    return pltpu.CompilerParams(dimension_semantics=sem, vmem_limit_bytes=VMEM_LIMIT)


def _ada_kernel(c_ref, w_ref, b_ref, o_ref):
    c = c_ref[...]
    ca = c * jax.nn.sigmoid(c)
    o_ref[...] = jnp.dot(ca, w_ref[...], preferred_element_type=F32,
                         precision=lax.Precision.HIGHEST) + b_ref[...]


def _ada(c8, w, b, tn=1024):
    rows, d = c8.shape
    n = w.shape[1]
    return pl.pallas_call(
        _ada_kernel,
        out_shape=jax.ShapeDtypeStruct((rows, n), F32),
        grid=(n // tn,),
        in_specs=[pl.BlockSpec((rows, d), lambda j: (0, 0)),
                  pl.BlockSpec((d, tn), lambda j: (0, j)),
                  pl.BlockSpec((1, tn), lambda j: (0, j))],
        out_specs=pl.BlockSpec((rows, tn), lambda j: (0, j)),
        compiler_params=_cparams(("arbitrary",)),
        name="ada",
    )(c8, w, b)


def _modulated_norm(x, g, sc, sh):
    ms = jnp.mean(x * x, axis=-1, keepdims=True)
    return (x * lax.rsqrt(ms + EPS) * g) * (1.0 + sc) + sh


def _normmod_kernel(x_ref, g_ref, mod_ref, o_ref):
    o_ref[...] = _modulated_norm(x_ref[...], g_ref[...], mod_ref[1:2, :], mod_ref[0:1, :]).astype(o_ref.dtype)


def _normmod(x2, g, mod3, seq, tm=512):
    m, d = x2.shape
    per_b = seq // tm
    return pl.pallas_call(
        _normmod_kernel,
        out_shape=jax.ShapeDtypeStruct((m, d), BF16),
        grid=(m // tm,),
        in_specs=[pl.BlockSpec((tm, d), lambda i: (i, 0)),
                  pl.BlockSpec((1, d), lambda i: (0, 0)),
                  pl.BlockSpec((None, 6, d), lambda i: (i // per_b, 0, 0))],
        out_specs=pl.BlockSpec((tm, d), lambda i: (i, 0)),
        compiler_params=_cparams(("parallel",)),
        name="normmod",
    )(x2, g, mod3)


def _head_norm_store(acc, gain_ref, o_ref, col0=0):
    for hh in range(acc.shape[1] // HEAD_DIM):
        a = acc[:, hh * HEAD_DIM:(hh + 1) * HEAD_DIM]
        ms = jnp.mean(a * a, axis=-1, keepdims=True)
        sl = slice(col0 + hh * HEAD_DIM, col0 + (hh + 1) * HEAD_DIM)
        o_ref[:, sl] = (a * lax.rsqrt(ms + EPS) * gain_ref[:, hh * HEAD_DIM:(hh + 1) * HEAD_DIM]).astype(o_ref.dtype)


def _proj_kernel(h_ref, w_ref, gain_ref, o_ref, *, mode):
    acc = jnp.dot(h_ref[...], w_ref[...], preferred_element_type=F32)
    if mode == "headnorm":
        _head_norm_store(acc, gain_ref, o_ref)
    elif mode == "sigmoid":
        o_ref[...] = jax.nn.sigmoid(acc).astype(o_ref.dtype)
    else:
        o_ref[...] = acc.astype(o_ref.dtype)


def _proj(h, w, gain, *, mode, out_dtype, tm=512, tn=1024, name="proj"):
    m, d = h.shape
    n = w.shape[1]
    tn = min(tn, n)
    if gain is None:
        gain = jnp.ones((1, n), F32)
    return pl.pallas_call(
        functools.partial(_proj_kernel, mode=mode),
        out_shape=jax.ShapeDtypeStruct((m, n), out_dtype),
        grid=(n // tn, m // tm),
        in_specs=[pl.BlockSpec((tm, d), lambda j, i: (i, 0)),
                  pl.BlockSpec((d, tn), lambda j, i: (0, j)),
                  pl.BlockSpec((1, tn), lambda j, i: (0, j))],
        out_specs=pl.BlockSpec((tm, tn), lambda j, i: (i, j)),
        compiler_params=_cparams(("parallel", "parallel")),
        name=name,
    )(h, w, gain)


def _ckv_kernel(h_ref, wc_ref, g_ref, wu_ref, gk_ref, k_ref, v_ref):
    c = jnp.dot(h_ref[...], wc_ref[...], preferred_element_type=F32)
    ms = jnp.mean(c * c, axis=-1, keepdims=True)
    cn = (c * lax.rsqrt(ms + EPS) * g_ref[...]).astype(BF16)
    kv = jnp.dot(cn, wu_ref[...], preferred_element_type=F32)
    _head_norm_store(kv[:, :DSA_W], gk_ref, k_ref)
    v_ref[...] = kv[:, DSA_W:].astype(v_ref.dtype)


def _ckv(h, wc, g, wu, gk, tm=512):
    m, d = h.shape
    return pl.pallas_call(
        _ckv_kernel,
        out_shape=(jax.ShapeDtypeStruct((m, DSA_W), BF16), jax.ShapeDtypeStruct((m, DSA_W), BF16)),
        grid=(m // tm,),
        in_specs=[pl.BlockSpec((tm, d), lambda i: (i, 0)),
                  pl.BlockSpec((d, KV_LORA), lambda i: (0, 0)),
                  pl.BlockSpec((1, KV_LORA), lambda i: (0, 0)),
                  pl.BlockSpec((KV_LORA, 2 * DSA_W), lambda i: (0, 0)),
                  pl.BlockSpec((1, DSA_W), lambda i: (0, 0))],
        out_specs=(pl.BlockSpec((tm, DSA_W), lambda i: (i, 0)), pl.BlockSpec((tm, DSA_W), lambda i: (i, 0))),
        compiler_params=_cparams(("parallel",)),
        name="ckv",
    )(h, wc, g, wu, gk)


def _cum_kernel(m_ref, bf_ref, o_ref, carry_ref):
    @pl.when(pl.program_id(1) == 0)
    def _():
        carry_ref[...] = jnp.zeros_like(carry_ref)

    z = m_ref[...] + bf_ref[...]
    lf = -(jnp.maximum(-z, 0.0) + jnp.log1p(jnp.exp(-jnp.abs(z))))
    tc = lf.shape[0]
    row = lax.broadcasted_iota(jnp.int32, lf.shape, 0)
    s = 1
    while s < tc:
        lf = lf + jnp.where(row >= s, pltpu.roll(lf, s, axis=0), 0.0)
        s *= 2
    out = lf + carry_ref[0:1, :]
    o_ref[...] = out
    carry_ref[...] = jnp.broadcast_to(out[tc - 1:tc, :], carry_ref.shape)


def _cumsum(misc3, bf, tc=512):
    b, seq, w = misc3.shape
    return pl.pallas_call(
        _cum_kernel,
        out_shape=jax.ShapeDtypeStruct((b, seq, w), F32),
        grid=(b, seq // tc),
        in_specs=[pl.BlockSpec((None, tc, w), lambda bb, i: (bb, i, 0)),
                  pl.BlockSpec((1, w), lambda bb, i: (0, 0))],
        out_specs=pl.BlockSpec((None, tc, w), lambda bb, i: (bb, i, 0)),
        scratch_shapes=[pltpu.VMEM((8, w), F32)],
        compiler_params=_cparams(("arbitrary", "arbitrary")),
        name="cumsum",
    )(misc3, bf)


def _softmax_update(s, v_h, m_scr, l_scr, acc_ref, h):
    m_prev = m_scr[h]
    m_new = jnp.maximum(m_prev, jnp.max(s, axis=1, keepdims=True))
    alpha = jnp.exp2(m_prev - m_new)
    p = jnp.exp2(s - m_new[:, :1]).astype(BF16)
    v_ones = jnp.concatenate([v_h, jnp.ones_like(v_h)], axis=1)
    pv = jnp.dot(p, v_ones, preferred_element_type=F32)
    l_scr[h] = alpha * l_scr[h] + pv[:, HEAD_DIM:]
    m_scr[h] = m_new
    sl = slice(h * HEAD_DIM, (h + 1) * HEAD_DIM)
    acc_ref[:, sl] = acc_ref[:, sl] * alpha + pv[:, :HEAD_DIM]


def _attn_init(m_scr, l_scr, acc_ref):
    m_scr[...] = jnp.full(m_scr.shape, NEG_INF, F32)
    l_scr[...] = jnp.zeros(l_scr.shape, F32)
    acc_ref[...] = jnp.zeros(acc_ref.shape, F32)


def _attn_finish(o_ref, l_scr, acc_ref, nheads):
    for h in range(nheads):
        sl = slice(h * HEAD_DIM, (h + 1) * HEAD_DIM)
        o_ref[:, sl] = (acc_ref[:, sl] / l_scr[h]).astype(o_ref.dtype)


def _qk(q_ref, k_ref, h, width=HEAD_DIM):
    sl = slice(h * width, (h + 1) * width)
    return lax.dot_general(q_ref[:, sl], k_ref[:, sl], (((1,), (1,)), ((), ())), preferred_element_type=F32)


AUG_W = 2 * HEAD_DIM
N_PIECES = 3


def _aug_kernel(q_ref, k_ref, cum_ref, selq_ref, selk_ref, cq_ref, ck_ref, qa_ref, ka_ref):
    lane = lax.broadcasted_iota(jnp.int32, cum_ref.shape, 1)
    c = jnp.where(lane < N_FOX_HEADS, cum_ref[...] * LOG2E, 0.0)
    hi = c.astype(BF16).astype(F32)
    r1 = c - hi
    mid = r1.astype(BF16).astype(F32)
    lo = (r1 - mid).astype(BF16).astype(F32)
    pieces = (hi + pltpu.roll(mid, N_FOX_HEADS, axis=1) + pltpu.roll(lo, 2 * N_FOX_HEADS, axis=1)).astype(BF16)
    for h in range(N_FOX_HEADS):
        sl = slice(h * HEAD_DIM, (h + 1) * HEAD_DIM)
        qa_ref[:, h * AUG_W:h * AUG_W + HEAD_DIM] = q_ref[:, sl]
        ka_ref[:, h * AUG_W:h * AUG_W + HEAD_DIM] = k_ref[:, sl]
        eq = jnp.dot(pieces, selq_ref[h], preferred_element_type=F32) + cq_ref[...]
        ek = jnp.dot(pieces, selk_ref[h], preferred_element_type=F32) + ck_ref[...]
        qa_ref[:, h * AUG_W + HEAD_DIM:(h + 1) * AUG_W] = eq.astype(BF16)
        ka_ref[:, h * AUG_W + HEAD_DIM:(h + 1) * AUG_W] = ek.astype(BF16)


def _aug_tables():
    selq = [[[0.0] * LANES for _ in range(LANES)] for _ in range(N_FOX_HEADS)]
    selk = [[[0.0] * LANES for _ in range(LANES)] for _ in range(N_FOX_HEADS)]
    for h in range(N_FOX_HEADS):
        for p in range(N_PIECES):
            selq[h][p * N_FOX_HEADS + h][p] = 1.0
            selk[h][p * N_FOX_HEADS + h][N_PIECES + p] = -1.0
    cq = [[1.0 if N_PIECES <= c < 2 * N_PIECES else 0.0 for c in range(LANES)]]
    ck = [[1.0 if c < N_PIECES else 0.0 for c in range(LANES)]]
    return (jnp.array(selq, BF16), jnp.array(selk, BF16), jnp.array(cq, F32), jnp.array(ck, F32))


def _augment(q, k, cum2, tm=512):
    m, w = q.shape
    selq, selk, cq, ck = _aug_tables()
    row = lambda i: (i, 0)
    const3 = lambda i: (0, 0, 0)
    const2 = lambda i: (0, 0)
    wa = N_FOX_HEADS * AUG_W
    return pl.pallas_call(
        _aug_kernel,
        out_shape=(jax.ShapeDtypeStruct((m, wa), BF16), jax.ShapeDtypeStruct((m, wa), BF16)),
        grid=(m // tm,),
        in_specs=[pl.BlockSpec((tm, w), row), pl.BlockSpec((tm, w), row), pl.BlockSpec((tm, LANES), row),
                  pl.BlockSpec((N_FOX_HEADS, LANES, LANES), const3), pl.BlockSpec((N_FOX_HEADS, LANES, LANES), const3),
                  pl.BlockSpec((1, LANES), const2), pl.BlockSpec((1, LANES), const2)],
        out_specs=(pl.BlockSpec((tm, wa), row), pl.BlockSpec((tm, wa), row)),
        compiler_params=_cparams(("parallel",)),
        name="augment",
    )(q, k, cum2, selq, selk, cq, ck)


ATT_TQ = 1024
ATT_TK = 512


def _causal_pairs(seq, tq, tk):
    r = tq // tk
    pairs = [(i, j) for i in range(seq // tq) for j in range(r * (i + 1))]
    return (jnp.array([p[0] for p in pairs], jnp.int32), jnp.array([p[1] for p in pairs], jnp.int32))


def _fox_kernel(qi_ref, kj_ref, q_ref, k_ref, v_ref, o_ref, m_scr, l_scr, acc_ref):
    t = pl.program_id(1)
    i, j = qi_ref[t], kj_ref[t]
    tq, tk = q_ref.shape[0], k_ref.shape[0]
    r = tq // tk

    @pl.when(j == 0)
    def _():
        _attn_init(m_scr, l_scr, acc_ref)

    def tile(masked):
        if masked:
            col = lax.broadcasted_iota(jnp.int32, (tq, tk), 1) + (j * tk - i * tq)
            keep = col <= lax.broadcasted_iota(jnp.int32, (tq, tk), 0)
        for h in range(N_FOX_HEADS):
            s = _qk(q_ref, k_ref, h, AUG_W)
            if masked:
                s = jnp.where(keep, s, NEG_INF)
            _softmax_update(s, v_ref[:, h * HEAD_DIM:(h + 1) * HEAD_DIM], m_scr, l_scr, acc_ref, h)

    @pl.when(j < r * i)
    def _():
        tile(False)

    @pl.when(j >= r * i)
    def _():
        tile(True)

    @pl.when(j == r * i + r - 1)
    def _():
        _attn_finish(o_ref, l_scr, acc_ref, N_FOX_HEADS)


def _fox(qa, ka, v):
    b, seq, w = v.shape
    wa = qa.shape[2]
    tq, tk = min(ATT_TQ, seq), min(ATT_TK, seq)
    qi, kj = _causal_pairs(seq, tq, tk)
    q_map = lambda bb, t, qi, kj: (bb, qi[t], 0)
    kv_map = lambda bb, t, qi, kj: (bb, kj[t], 0)
    return pl.pallas_call(
        _fox_kernel,
        out_shape=jax.ShapeDtypeStruct((b, seq, w), BF16),
        grid_spec=pltpu.PrefetchScalarGridSpec(
            num_scalar_prefetch=2,
            grid=(b, qi.shape[0]),
            in_specs=[pl.BlockSpec((None, tq, wa), q_map),
                      pl.BlockSpec((None, tk, wa), kv_map),
                      pl.BlockSpec((None, tk, w), kv_map)],
            out_specs=pl.BlockSpec((None, tq, w), q_map),
            scratch_shapes=[pltpu.VMEM((N_FOX_HEADS, tq, LANES), F32),
                            pltpu.VMEM((N_FOX_HEADS, tq, LANES), F32),
                            pltpu.VMEM((tq, w), F32)]),
        compiler_params=_cparams(("parallel", "arbitrary")),
        name="fox",
    )(qi, kj, qa, ka, v)


IDX_ROWS = 128
IDX_CH = 256


def _index_kernel(qi_ref, misc_ref, kit_ref, o_ref, keys_scr, wb_scr, *, topk):
    i = pl.program_id(1)
    seq = kit_ref.shape[1]
    nsub = IDX_CH // LANES
    t0 = i * IDX_ROWS
    npair = (t0 + IDX_ROWS + 2 * IDX_CH - 1) // (2 * IDX_CH)

    def chunk_loop(fn, init):
        return lax.fori_loop(0, npair, lambda cp, carry: fn(2 * cp + 1, fn(2 * cp, carry)), init)
    w = misc_ref[:, MISC_WI:MISC_WI + N_IDX_HEADS] * (N_IDX_HEADS ** -0.5 * IDX_DIM ** -0.5)
    for jh in range(N_IDX_HEADS):
        wb_scr[jh] = jnp.broadcast_to(w[:, jh:jh + 1], (IDX_ROWS, LANES))
    row = t0 + lax.broadcasted_iota(jnp.int32, (IDX_ROWS, LANES), 0)
    lane = lax.broadcasted_iota(jnp.int32, (IDX_ROWS, LANES), 1)

    def score_chunk(c, carry):
        c0 = pl.multiple_of(c * IDX_CH, IDX_CH)
        kc = kit_ref[:, pl.ds(c0, IDX_CH)]
        z = jnp.zeros_like(kc)
        rhs = jnp.concatenate([jnp.concatenate([kc, z], axis=1), jnp.concatenate([z, kc], axis=1)], axis=0)
        accs = [jnp.zeros((IDX_ROWS, LANES), F32) for _ in range(nsub)]
        for p in range(N_IDX_HEADS // 2):
            s2 = jnp.dot(qi_ref[:, p * 2 * IDX_DIM:(p + 1) * 2 * IDX_DIM], rhs, preferred_element_type=F32)
            wa, wb = wb_scr[2 * p], wb_scr[2 * p + 1]
            for u in range(nsub):
                accs[u] = (accs[u] + wa * jnp.maximum(s2[:, u * LANES:(u + 1) * LANES], 0.0)
                           + wb * jnp.maximum(s2[:, IDX_CH + u * LANES:IDX_CH + (u + 1) * LANES], 0.0))
        for u in range(nsub):
            bits = pltpu.bitcast(accs[u], jnp.int32)
            key = bits ^ ((bits >> 31) & 0x7FFFFFFF)
            col = c0 + u * LANES + lane
            keys_scr[:, pl.ds(pl.multiple_of(c0 + u * LANES, LANES), LANES)] = jnp.where(col <= row, key, INT_MIN)
        return carry

    chunk_loop(score_chunk, 0)

    def count_ge(cand):
        def body(c, acc):
            c0 = pl.multiple_of(c * IDX_CH, IDX_CH)
            blk = keys_scr[:, pl.ds(c0, IDX_CH)]
            for u in range(nsub):
                acc = acc + jnp.where(blk[:, u * LANES:(u + 1) * LANES] >= cand, 1.0, 0.0)
            return acc
        acc = chunk_loop(body, jnp.zeros((IDX_ROWS, LANES), F32))
        return jnp.broadcast_to(jnp.sum(acc, axis=1, keepdims=True), (IDX_ROWS, LANES))

    def lane_all(x, op):
        s = LANES // 2
        while s >= 1:
            x = op(x, pltpu.roll(x, s, axis=1))
            s //= 2
        return x

    def gmax_body(c, gs):
        c0 = pl.multiple_of(c * IDX_CH, IDX_CH)
        blk = keys_scr[:, pl.ds(c0, IDX_CH)]
        return tuple(jnp.maximum(g, blk[:, u * LANES:(u + 1) * LANES]) for u, g in enumerate(gs))

    gs = chunk_loop(gmax_body, tuple(jnp.full((IDX_ROWS, LANES), INT_MIN, jnp.int32) for _ in range(nsub)))
    gmin, gmax = gs[0], gs[0]
    for g in gs[1:]:
        gmin, gmax = jnp.minimum(gmin, g), jnp.maximum(gmax, g)
    short = row < topk

    def all_true(flag):
        return (jnp.min(jnp.where(flag, 1.0, 0.0)) > 0.5).astype(jnp.int32)

    def settled(lo, hi, clo):
        return all_true(short | (clo == topk) | (hi - lo == 1))

    def bisect(state):
        it, lo, hi, clo, chi, _ = state
        mid = (lo >> 1) + (hi >> 1) + (lo & hi & 1)
        cnt = count_ge(mid)
        ge = cnt >= topk
        lo, hi = jnp.where(ge, mid, lo), jnp.where(ge, hi, mid)
        clo, chi = jnp.where(ge, cnt, clo), jnp.where(ge, chi, cnt)
        return it + 1, lo, hi, clo, chi, settled(lo, hi, clo)

    lo0 = lane_all(gmin, jnp.minimum)
    hi0 = lane_all(gmax, jnp.maximum) + 1
    clo0 = jnp.full((IDX_ROWS, LANES), -1.0, F32)
    chi0 = jnp.zeros((IDX_ROWS, LANES), F32)
    _, lo, _, clo, chi, _ = lax.while_loop(lambda st: (st[5] == 0) & (st[0] < 34), bisect,
                                           (jnp.int32(0), lo0, hi0, clo0, chi0, settled(lo0, hi0, clo0)))
    thr = jnp.where(short, INT_MIN + 1, jnp.maximum(lo, INT_MIN + 1))
    tied = jnp.logical_not(short) & (clo != topk)
    no_ties = all_true(jnp.logical_not(tied))

    def emit_chunks(select):
        def emit(c, carry):
            c0 = pl.multiple_of(c * IDX_CH, IDX_CH)
            blk = keys_scr[:, pl.ds(c0, IDX_CH)]
            for u in range(nsub):
                sel = select(blk[:, u * LANES:(u + 1) * LANES], c0 + u * LANES + lane)
                o_ref[:, pl.ds(pl.multiple_of(c0 + u * LANES, LANES), LANES)] = jnp.where(sel, 0.0, NEG_INF).astype(o_ref.dtype)
            return carry
        chunk_loop(emit, 0)

    @pl.when(no_ties == 1)
    def _():
        emit_chunks(lambda key, col: key >= thr)

    @pl.when(no_ties == 0)
    def _():
        need = topk - chi

        def count_tied_upto(col_max):
            def body(c, acc):
                c0 = pl.multiple_of(c * IDX_CH, IDX_CH)
                blk = keys_scr[:, pl.ds(c0, IDX_CH)]
                for u in range(nsub):
                    hit = (blk[:, u * LANES:(u + 1) * LANES] == lo) & (c0 + u * LANES + lane <= col_max)
                    acc = acc + jnp.where(hit, 1.0, 0.0)
                return acc
            acc = chunk_loop(body, jnp.zeros((IDX_ROWS, LANES), F32))
            return jnp.broadcast_to(jnp.sum(acc, axis=1, keepdims=True), (IDX_ROWS, LANES))

        def col_step(_, state):
            below, last = state
            mid = (below + last) >> 1
            enough = count_tied_upto(mid) >= need
            return jnp.where(enough, below, mid), jnp.where(enough, mid, last)

        steps = max(seq - 1, 1).bit_length() + 1
        _, last = lax.fori_loop(0, steps, col_step, (jnp.full((IDX_ROWS, LANES), -1, jnp.int32),
                                                      jnp.full((IDX_ROWS, LANES), seq - 1, jnp.int32)))
        untied = jnp.logical_not(tied)
        emit_chunks(lambda key, col: (tied & ((key > lo) | ((key == lo) & (col <= last)))) | (untied & (key >= thr)))

    def fill(c, carry):
        c0 = pl.multiple_of(c * IDX_CH, IDX_CH)
        o_ref[:, pl.ds(c0, IDX_CH)] = jnp.full((IDX_ROWS, IDX_CH), NEG_INF, o_ref.dtype)
        return carry

    lax.fori_loop(2 * npair, seq // IDX_CH, fill, 0)


def _index(qi, misc3, kit, topk):
    b, seq, w = qi.shape
    return pl.pallas_call(
        functools.partial(_index_kernel, topk=topk),
        out_shape=jax.ShapeDtypeStruct((b, seq, seq), BF16),
        grid=(b, seq // IDX_ROWS),
        in_specs=[pl.BlockSpec((None, IDX_ROWS, w), lambda bb, i: (bb, i, 0)),
                  pl.BlockSpec((None, IDX_ROWS, LANES), lambda bb, i: (bb, i, 0)),
                  pl.BlockSpec((None, IDX_DIM, seq), lambda bb, i: (bb, 0, 0))],
        out_specs=pl.BlockSpec((None, IDX_ROWS, seq), lambda bb, i: (bb, i, 0)),
        scratch_shapes=[pltpu.VMEM((IDX_ROWS, seq), jnp.int32),
                        pltpu.VMEM((N_IDX_HEADS, IDX_ROWS, LANES), F32)],
        compiler_params=_cparams(("parallel", "arbitrary")),
        name="index",
    )(qi, misc3, kit)


def _dsa_kernel(qi_ref, kj_ref, tab_ref, q_ref, k_ref, v_ref, b_ref, o_ref, m_scr, l_scr, acc_ref, toe_scr, s_scr):
    t = pl.program_id(1)
    i, j = qi_ref[t], kj_ref[t]
    tq, tk = q_ref.shape[0], k_ref.shape[0]
    r = tq // tk
    nbq, nbk = tq // LANES, tk // LANES

    @pl.when(j == 0)
    def _():
        _attn_init(m_scr, l_scr, acc_ref)

    @pl.when(t == 0)
    def _():
        row = lax.broadcasted_iota(jnp.int32, (LANES, LANES), 0)
        col = lax.broadcasted_iota(jnp.int32, (LANES, LANES), 1)
        d_diag, d_sub = row - col, LANES + row - col
        for h in range(N_DSA_HEADS):
            def body(d, carry, h=h):
                td, ts = carry
                val = tab_ref[h, d]
                return jnp.where(d_diag == d, val, td), jnp.where(d_sub == d, val, ts)
            zero = jnp.zeros((LANES, LANES), F32)
            td, ts = lax.fori_loop(0, LANES, body, (zero, zero))
            toe_scr[h, 0] = td
            toe_scr[h, 1] = ts

    def tile(e):
        near = [] if e is None else [(a, c, a - c - e * nbk) for a in range(nbq) for c in range(nbk)
                                     if a - c - e * nbk in (0, 1)]
        bias = b_ref[...].astype(F32)
        for h in range(N_DSA_HEADS):
            s = _qk(q_ref, k_ref, h) + bias
            if near:
                s_scr[...] = s
                for a, c, which in near:
                    s_scr[a * LANES:(a + 1) * LANES, c * LANES:(c + 1) * LANES] += toe_scr[h, which]
                s = s_scr[...]
            _softmax_update(s, v_ref[:, h * HEAD_DIM:(h + 1) * HEAD_DIM], m_scr, l_scr, acc_ref, h)

    @pl.when(j < r * i - 1)
    def _():
        tile(None)

    for e in range(-1, r):
        @pl.when(j == r * i + e)
        def _(e=e):
            tile(e)

    @pl.when(j == r * i + r - 1)
    def _():
        _attn_finish(o_ref, l_scr, acc_ref, N_DSA_HEADS)


def _dsa(tab, q, k, v, bias):
    b, seq, w = q.shape
    tq, tk = min(ATT_TQ, seq), min(ATT_TK, seq)
    qi, kj = _causal_pairs(seq, tq, tk)
    q_map = lambda bb, t, qi, kj: (bb, qi[t], 0)
    kv_map = lambda bb, t, qi, kj: (bb, kj[t], 0)
    return pl.pallas_call(
        _dsa_kernel,
        out_shape=jax.ShapeDtypeStruct((b, seq, w), BF16),
        grid_spec=pltpu.PrefetchScalarGridSpec(
            num_scalar_prefetch=2,
            grid=(b, qi.shape[0]),
            in_specs=[pl.BlockSpec(memory_space=pltpu.SMEM),
                      pl.BlockSpec((None, tq, w), q_map),
                      pl.BlockSpec((None, tk, w), kv_map),
                      pl.BlockSpec((None, tk, w), kv_map),
                      pl.BlockSpec((None, tq, tk), lambda bb, t, qi, kj: (bb, qi[t], kj[t]))],
            out_specs=pl.BlockSpec((None, tq, w), q_map),
            scratch_shapes=[pltpu.VMEM((N_DSA_HEADS, tq, LANES), F32),
                            pltpu.VMEM((N_DSA_HEADS, tq, LANES), F32),
                            pltpu.VMEM((tq, w), F32),
                            pltpu.VMEM((N_DSA_HEADS, 2, LANES, LANES), F32),
                            pltpu.VMEM((tq, tk), F32)]),
        compiler_params=_cparams(("arbitrary", "arbitrary")),
        name="dsa",
    )(qi, kj, tab, q, k, v, bias)


def _merge_kernel(af_ref, ad_ref, ga_ref, gb_ref, x_ref, mod_ref, wof_ref, wod_ref, wo_ref, g2_ref,
                  x1_ref, h2_ref):
    yf = jnp.dot(af_ref[...], wof_ref[...], preferred_element_type=F32)
    yd = jnp.dot(ad_ref[...], wod_ref[...], preferred_element_type=F32)
    merged = ga_ref[...].astype(F32) * yf + gb_ref[...].astype(F32) * yd
    o = jnp.dot(merged.astype(BF16), wo_ref[...], preferred_element_type=F32)
    x1 = x_ref[...] + mod_ref[2:3, :] * o
    x1_ref[...] = x1
    h2_ref[...] = _modulated_norm(x1, g2_ref[...], mod_ref[4:5, :], mod_ref[3:4, :]).astype(h2_ref.dtype)


def _merge(af, ad, ga, gb, x2, mod3, wof, wod, wo, g2, seq, tm=256):
    m, d = x2.shape
    per_b = seq // tm
    row = lambda i: (i, 0)
    const = lambda i: (0, 0)
    return pl.pallas_call(
        _merge_kernel,
        out_shape=(jax.ShapeDtypeStruct((m, d), F32), jax.ShapeDtypeStruct((m, d), BF16)),
        grid=(m // tm,),
        in_specs=[pl.BlockSpec((tm, FOX_W), row), pl.BlockSpec((tm, DSA_W), row),
                  pl.BlockSpec((tm, d), row), pl.BlockSpec((tm, d), lambda i: (i, 1)), pl.BlockSpec((tm, d), row),
                  pl.BlockSpec((None, 6, d), lambda i: (i // per_b, 0, 0)),
                  pl.BlockSpec((FOX_W, d), const), pl.BlockSpec((DSA_W, d), const), pl.BlockSpec((d, d), const),
                  pl.BlockSpec((1, d), const)],
        out_specs=(pl.BlockSpec((tm, d), row), pl.BlockSpec((tm, d), row)),
        compiler_params=_cparams(("parallel",)),
        name="merge",
    )(af, ad, ga, gb, x2, mod3, wof, wod, wo, g2)


FFN_HALO = 16


def _ffn_kernel(h_ref, halo_ref, wa_ref, wb_ref, cwa_ref, cwb_ref, cba_ref, cbb_ref, wout_ref, x1_ref, mod_ref,
                o_ref, hext_scr, acc_ref, *, per_b):
    i = pl.program_id(0)
    f = pl.program_id(1)
    tm = h_ref.shape[0]

    @pl.when(f == 0)
    def _():
        first = (i % per_b) == 0
        hext_scr[0:FFN_HALO, :] = jnp.where(first, jnp.zeros_like(halo_ref[...]), halo_ref[...])
        hext_scr[FFN_HALO:, :] = h_ref[...]
        acc_ref[...] = jnp.zeros_like(acc_ref)

    hext = hext_scr[...]

    def conv(w_ref, cw_ref, cb_ref):
        u = jnp.dot(hext, w_ref[...], preferred_element_type=F32)
        y = cw_ref[2:3, :] * u + cw_ref[1:2, :] * pltpu.roll(u, 1, axis=0) + cw_ref[0:1, :] * pltpu.roll(u, 2, axis=0)
        return y[FFN_HALO:, :] + cb_ref[...]

    ya = conv(wa_ref, cwa_ref, cba_ref)
    yb = conv(wb_ref, cwb_ref, cbb_ref)
    act = (ya * jax.nn.sigmoid(ya) * yb).astype(BF16)
    acc_ref[...] += jnp.dot(act, wout_ref[...], preferred_element_type=F32)

    @pl.when(f == pl.num_programs(1) - 1)
    def _():
        o_ref[...] = x1_ref[...] + mod_ref[5:6, :] * acc_ref[...]


def _ffn(h2, w_in, conv_w, conv_b, w_out, x1, mod3, seq, tm=512, tf=512):
    m, d = h2.shape
    dff = w_out.shape[0]
    nf = dff // tf
    per_b = seq // tm
    hb = tm // FFN_HALO
    return pl.pallas_call(
        functools.partial(_ffn_kernel, per_b=per_b),
        out_shape=jax.ShapeDtypeStruct((m, d), F32),
        grid=(m // tm, nf),
        in_specs=[pl.BlockSpec((tm, d), lambda i, f: (i, 0)),
                  pl.BlockSpec((FFN_HALO, d), lambda i, f: (jnp.maximum(i * hb - 1, 0), 0)),
                  pl.BlockSpec((d, tf), lambda i, f: (0, f)),
                  pl.BlockSpec((d, tf), lambda i, f: (0, f + nf)),
                  pl.BlockSpec((CONV_WIDTH, tf), lambda i, f: (0, f)),
                  pl.BlockSpec((CONV_WIDTH, tf), lambda i, f: (0, f + nf)),
                  pl.BlockSpec((1, tf), lambda i, f: (0, f)),
                  pl.BlockSpec((1, tf), lambda i, f: (0, f + nf)),
                  pl.BlockSpec((tf, d), lambda i, f: (f, 0)),
                  pl.BlockSpec((tm, d), lambda i, f: (i, 0)),
                  pl.BlockSpec((None, 6, d), lambda i, f: (i // per_b, 0, 0))],
        out_specs=pl.BlockSpec((tm, d), lambda i, f: (i, 0)),
        scratch_shapes=[pltpu.VMEM((tm + FFN_HALO, d), BF16), pltpu.VMEM((tm, d), F32)],
        compiler_params=_cparams(("parallel", "arbitrary")),
        name="ffn",
    )(h2, h2, w_in, w_in, conv_w, conv_w, conv_b, conv_b, w_out, x1, mod3)


def _t5_bucket(n):
    n = jnp.maximum(n, 0)
    max_exact = N_BUCKETS // 2
    nf = jnp.maximum(n, 1).astype(F32)
    large = max_exact + (jnp.log(nf / max_exact) / math.log(MAX_DISTANCE / max_exact)
                         * (N_BUCKETS - max_exact)).astype(jnp.int32)
    large = jnp.minimum(large, N_BUCKETS - 1)
    return jnp.where(n < max_exact, n, large)


def _layer(x, c8, w_ada, b_ada, norm1_g, w_in, b_forget, q_norm_fox, k_norm_fox, kv_norm_g, w_ukv, q_norm_dsa,
           k_norm_dsa, w_out_fox, w_out_dsa, w_out, norm2_g, w_ffn_in, conv_w, conv_b, w_ffn_out, rel_bias):
    b, seq, d = x.shape
    m = b * seq
    topk = min(TOPK_MAX, seq // 4)
    x2 = x.reshape(m, d)

    mod3 = _ada(c8, w_ada, b_ada.reshape(1, -1))[:b].reshape(b, 6, d)
    h1 = _normmod(x2, norm1_g.reshape(1, d), mod3, seq)

    o = 0
    cols = {}
    for name, size in (("qf", FOX_W), ("kf", FOX_W), ("vf", FOX_W), ("fg", N_FOX_HEADS), ("qd", DSA_W),
                       ("ckv", KV_LORA), ("qi", N_IDX_HEADS * IDX_DIM), ("ki", IDX_DIM), ("wi", N_IDX_HEADS),
                       ("ga", d), ("gb", d)):
        cols[name] = w_in[:, o:o + size]
        o += size
    wb = lambda a: a.astype(BF16)
    scale = HEAD_DIM ** -0.5 * LOG2E
    tile_h = lambda g, nh: jnp.tile(g.reshape(1, HEAD_DIM), (1, nh))

    qf = _proj(h1, wb(cols["qf"]), tile_h(q_norm_fox, N_FOX_HEADS) * scale, mode="headnorm", out_dtype=BF16, name="proj_qf")
    kf = _proj(h1, wb(cols["kf"]), tile_h(k_norm_fox, N_FOX_HEADS), mode="headnorm", out_dtype=BF16, name="proj_kf")
    vf = _proj(h1, wb(cols["vf"]), None, mode="plain", out_dtype=BF16, name="proj_vf")
    qd = _proj(h1, wb(cols["qd"]), tile_h(q_norm_dsa, N_DSA_HEADS) * scale, mode="headnorm", out_dtype=BF16, name="proj_qd")
    qi = _proj(h1, wb(cols["qi"]), None, mode="plain", out_dtype=BF16, name="proj_qi")
    gates = _proj(h1, wb(jnp.concatenate([cols["ga"], cols["gb"]], axis=1)), None, mode="sigmoid", out_dtype=BF16,
                  name="proj_gates")
    pad = jnp.zeros((d, LANES - N_FOX_HEADS - N_IDX_HEADS - IDX_DIM), w_in.dtype)
    misc = _proj(h1, wb(jnp.concatenate([cols["fg"], cols["wi"], cols["ki"], pad], axis=1)), None, mode="plain",
                 out_dtype=F32, name="proj_misc")
    kd, vd = _ckv(h1, wb(cols["ckv"]), kv_norm_g.reshape(1, KV_LORA), wb(w_ukv), tile_h(k_norm_dsa, N_DSA_HEADS))

    misc3 = misc.reshape(b, seq, LANES)
    bf = jnp.zeros((1, LANES), F32).at[0, :N_FOX_HEADS].set(b_forget.astype(F32))
    cum = _cumsum(misc3, bf)
    r3 = lambda a: a.reshape(b, seq, -1)
    qa, ka = _augment(qf, kf, cum.reshape(m, LANES))
    a_fox = _fox(r3(qa), r3(ka), r3(vf))

    kit = jnp.swapaxes(misc3[:, :, MISC_KI:MISC_KI + IDX_DIM], 1, 2).astype(BF16)
    sel_bias = _index(r3(qi), misc3, kit, topk)
    by_dist = rel_bias[_t5_bucket(jnp.arange(LANES, dtype=jnp.int32))] - rel_bias[N_BUCKETS - 1][None, :]
    a_dsa = _dsa((by_dist.T * LOG2E).astype(F32), r3(qd), r3(kd), r3(vd), sel_bias)

    x1, h2 = _merge(a_fox.reshape(m, FOX_W), a_dsa.reshape(m, DSA_W), gates, gates, x2, mod3,
                    wb(w_out_fox), wb(w_out_dsa), wb(w_out), norm2_g.reshape(1, d), seq)
    out = _ffn(h2, wb(w_ffn_in), conv_w, conv_b.reshape(1, -1), wb(w_ffn_out), x1, mod3, seq)
    return out.reshape(b, seq, d)


def kernel(x, c, w_ada, b_ada, norm1_g, w_in, b_forget, q_norm_fox, k_norm_fox, kv_norm_g, w_ukv, q_norm_dsa, k_norm_dsa, w_out_fox, w_out_dsa, w_out, norm2_g, w_ffn_in, conv_w, conv_b, w_ffn_out, rel_bias):
    b = x.shape[0]
    c8 = jnp.zeros((8, c.shape[1]), c.dtype).at[:b].set(c)
    for l in range(w_ada.shape[0]):
        x = _layer(x, c8, w_ada[l], b_ada[l], norm1_g[l], w_in[l], b_forget[l], q_norm_fox[l], k_norm_fox[l],
                   kv_norm_g[l], w_ukv[l], q_norm_dsa[l], k_norm_dsa[l], w_out_fox[l], w_out_dsa[l], w_out[l],
                   norm2_g[l], w_ffn_in[l], conv_w[l], conv_b[l], w_ffn_out[l], rel_bias)
    return x
```

```python
import functools
import math

import jax
import jax.numpy as jnp
from jax import lax
from jax.experimental import pallas as pl
from jax.experimental.pallas import tpu as pltpu

HEAD_DIM = 128
N_FOX_HEADS = 8
N_DSA_HEADS = 8
FOX_W = N_FOX_HEADS * HEAD_DIM
DSA_W = N_DSA_HEADS * HEAD_DIM
KV_LORA = 256
N_IDX_HEADS = 16
IDX_DIM = 64
TOPK_MAX = 256
N_BUCKETS = 32
MAX_DISTANCE = 128
CONV_WIDTH = 3
EPS = 1e-6
NEG_INF = -1e30
LOG2E = 1.4426950408889634

LANES = 128
INT_MIN = -(2 ** 31)
VMEM_LIMIT = 56 * 1024 * 1024

F32 = jnp.float32
BF16 = jnp.bfloat16

MISC_FG = 0
MISC_WI = N_FOX_HEADS
MISC_KI = MISC_WI + N_IDX_HEADS


def _cparams(sem):
    return pltpu.CompilerParams(dimension_semantics=sem, vmem_limit_bytes=VMEM_LIMIT)


def _ada_kernel(c_ref, w_ref, b_ref, o_ref):
    c = c_ref[...]
    ca = c * jax.nn.sigmoid(c)
    o_ref[...] = jnp.dot(ca, w_ref[...], preferred_element_type=F32,
                         precision=lax.Precision.HIGHEST) + b_ref[...]


def _ada(c8, w, b, tn=1024):
    rows, d = c8.shape
    n = w.shape[1]
    return pl.pallas_call(
        _ada_kernel,
        out_shape=jax.ShapeDtypeStruct((rows, n), F32),
        grid=(n // tn,),
        in_specs=[pl.BlockSpec((rows, d), lambda j: (0, 0)),
                  pl.BlockSpec((d, tn), lambda j: (0, j)),
                  pl.BlockSpec((1, tn), lambda j: (0, j))],
        out_specs=pl.BlockSpec((rows, tn), lambda j: (0, j)),
        compiler_params=_cparams(("arbitrary",)),
        name="ada",
    )(c8, w, b)


def _modulated_norm(x, g, sc, sh):
    ms = jnp.mean(x * x, axis=-1, keepdims=True)
    return (x * lax.rsqrt(ms + EPS) * g) * (1.0 + sc) + sh


def _normmod_kernel(x_ref, g_ref, mod_ref, o_ref):
    o_ref[...] = _modulated_norm(x_ref[...], g_ref[...], mod_ref[1:2, :], mod_ref[0:1, :]).astype(o_ref.dtype)


def _normmod(x2, g, mod3, seq, tm=512):
    m, d = x2.shape
    per_b = seq // tm
    return pl.pallas_call(
        _normmod_kernel,
        out_shape=jax.ShapeDtypeStruct((m, d), BF16),
        grid=(m // tm,),
        in_specs=[pl.BlockSpec((tm, d), lambda i: (i, 0)),
                  pl.BlockSpec((1, d), lambda i: (0, 0)),
                  pl.BlockSpec((None, 6, d), lambda i: (i // per_b, 0, 0))],
        out_specs=pl.BlockSpec((tm, d), lambda i: (i, 0)),
        compiler_params=_cparams(("parallel",)),
        name="normmod",
    )(x2, g, mod3)


def _head_norm_store(acc, gain_ref, o_ref, col0=0):
    for hh in range(acc.shape[1] // HEAD_DIM):
        a = acc[:, hh * HEAD_DIM:(hh + 1) * HEAD_DIM]
        ms = jnp.mean(a * a, axis=-1, keepdims=True)
        sl = slice(col0 + hh * HEAD_DIM, col0 + (hh + 1) * HEAD_DIM)
        o_ref[:, sl] = (a * lax.rsqrt(ms + EPS) * gain_ref[:, hh * HEAD_DIM:(hh + 1) * HEAD_DIM]).astype(o_ref.dtype)


def _proj_kernel(h_ref, w_ref, gain_ref, o_ref, *, mode):
    acc = jnp.dot(h_ref[...], w_ref[...], preferred_element_type=F32)
    if mode == "headnorm":
        _head_norm_store(acc, gain_ref, o_ref)
    elif mode == "sigmoid":
        o_ref[...] = jax.nn.sigmoid(acc).astype(o_ref.dtype)
    else:
        o_ref[...] = acc.astype(o_ref.dtype)


def _proj(h, w, gain, *, mode, out_dtype, tm=512, tn=1024, name="proj"):
    m, d = h.shape
    n = w.shape[1]
    tn = min(tn, n)
    if gain is None:
        gain = jnp.ones((1, n), F32)
    return pl.pallas_call(
        functools.partial(_proj_kernel, mode=mode),
        out_shape=jax.ShapeDtypeStruct((m, n), out_dtype),
        grid=(n // tn, m // tm),
        in_specs=[pl.BlockSpec((tm, d), lambda j, i: (i, 0)),
                  pl.BlockSpec((d, tn), lambda j, i: (0, j)),
                  pl.BlockSpec((1, tn), lambda j, i: (0, j))],
        out_specs=pl.BlockSpec((tm, tn), lambda j, i: (i, j)),
        compiler_params=_cparams(("parallel", "parallel")),
        name=name,
    )(h, w, gain)


def _ckv_kernel(h_ref, wc_ref, g_ref, wu_ref, gk_ref, k_ref, v_ref):
    c = jnp.dot(h_ref[...], wc_ref[...], preferred_element_type=F32)
    ms = jnp.mean(c * c, axis=-1, keepdims=True)
    cn = (c * lax.rsqrt(ms + EPS) * g_ref[...]).astype(BF16)
    kv = jnp.dot(cn, wu_ref[...], preferred_element_type=F32)
    _head_norm_store(kv[:, :DSA_W], gk_ref, k_ref)
    v_ref[...] = kv[:, DSA_W:].astype(v_ref.dtype)


def _ckv(h, wc, g, wu, gk, tm=512):
    m, d = h.shape
    return pl.pallas_call(
        _ckv_kernel,
        out_shape=(jax.ShapeDtypeStruct((m, DSA_W), BF16), jax.ShapeDtypeStruct((m, DSA_W), BF16)),
        grid=(m // tm,),
        in_specs=[pl.BlockSpec((tm, d), lambda i: (i, 0)),
                  pl.BlockSpec((d, KV_LORA), lambda i: (0, 0)),
                  pl.BlockSpec((1, KV_LORA), lambda i: (0, 0)),
                  pl.BlockSpec((KV_LORA, 2 * DSA_W), lambda i: (0, 0)),
                  pl.BlockSpec((1, DSA_W), lambda i: (0, 0))],
        out_specs=(pl.BlockSpec((tm, DSA_W), lambda i: (i, 0)), pl.BlockSpec((tm, DSA_W), lambda i: (i, 0))),
        compiler_params=_cparams(("parallel",)),
        name="ckv",
    )(h, wc, g, wu, gk)


def _cum_kernel(m_ref, bf_ref, o_ref, carry_ref):
    @pl.when(pl.program_id(1) == 0)
    def _():
        carry_ref[...] = jnp.zeros_like(carry_ref)

    z = m_ref[...] + bf_ref[...]
    lf = -(jnp.maximum(-z, 0.0) + jnp.log1p(jnp.exp(-jnp.abs(z))))
    tc = lf.shape[0]
    row = lax.broadcasted_iota(jnp.int32, lf.shape, 0)
    s = 1
    while s < tc:
        lf = lf + jnp.where(row >= s, pltpu.roll(lf, s, axis=0), 0.0)
        s *= 2
    out = lf + carry_ref[0:1, :]
    o_ref[...] = out
    carry_ref[...] = jnp.broadcast_to(out[tc - 1:tc, :], carry_ref.shape)


def _cumsum(misc3, bf, tc=512):
    b, seq, w = misc3.shape
    return pl.pallas_call(
        _cum_kernel,
        out_shape=jax.ShapeDtypeStruct((b, seq, w), F32),
        grid=(b, seq // tc),
        in_specs=[pl.BlockSpec((None, tc, w), lambda bb, i: (bb, i, 0)),
                  pl.BlockSpec((1, w), lambda bb, i: (0, 0))],
        out_specs=pl.BlockSpec((None, tc, w), lambda bb, i: (bb, i, 0)),
        scratch_shapes=[pltpu.VMEM((8, w), F32)],
        compiler_params=_cparams(("arbitrary", "arbitrary")),
        name="cumsum",
    )(misc3, bf)


def _softmax_update(s, v_h, m_scr, l_scr, acc_ref, h):
    m_prev = m_scr[h]
    m_new = jnp.maximum(m_prev, jnp.max(s, axis=1, keepdims=True))
    alpha = jnp.exp2(m_prev - m_new)
    p = jnp.exp2(s - m_new[:, :1]).astype(BF16)
    v_ones = jnp.concatenate([v_h, jnp.ones_like(v_h)], axis=1)
    pv = jnp.dot(p, v_ones, preferred_element_type=F32)
    l_scr[h] = alpha * l_scr[h] + pv[:, HEAD_DIM:]
    m_scr[h] = m_new
    sl = slice(h * HEAD_DIM, (h + 1) * HEAD_DIM)
    acc_ref[:, sl] = acc_ref[:, sl] * alpha + pv[:, :HEAD_DIM]


def _attn_init(m_scr, l_scr, acc_ref):
    m_scr[...] = jnp.full(m_scr.shape, NEG_INF, F32)
    l_scr[...] = jnp.zeros(l_scr.shape, F32)
    acc_ref[...] = jnp.zeros(acc_ref.shape, F32)


def _attn_finish(o_ref, l_scr, acc_ref, nheads):
    for h in range(nheads):
        sl = slice(h * HEAD_DIM, (h + 1) * HEAD_DIM)
        o_ref[:, sl] = (acc_ref[:, sl] / l_scr[h]).astype(o_ref.dtype)


def _qk(q_ref, k_ref, h, width=HEAD_DIM):
    sl = slice(h * width, (h + 1) * width)
    return lax.dot_general(q_ref[:, sl], k_ref[:, sl], (((1,), (1,)), ((), ())), preferred_element_type=F32)


AUG_W = 2 * HEAD_DIM
N_PIECES = 3


def _aug_kernel(q_ref, k_ref, cum_ref, selq_ref, selk_ref, cq_ref, ck_ref, qa_ref, ka_ref):
    lane = lax.broadcasted_iota(jnp.int32, cum_ref.shape, 1)
    c = jnp.where(lane < N_FOX_HEADS, cum_ref[...] * LOG2E, 0.0)
    hi = c.astype(BF16).astype(F32)
    r1 = c - hi
    mid = r1.astype(BF16).astype(F32)
    lo = (r1 - mid).astype(BF16).astype(F32)
    pieces = (hi + pltpu.roll(mid, N_FOX_HEADS, axis=1) + pltpu.roll(lo, 2 * N_FOX_HEADS, axis=1)).astype(BF16)
    for h in range(N_FOX_HEADS):
        sl = slice(h * HEAD_DIM, (h + 1) * HEAD_DIM)
        qa_ref[:, h * AUG_W:h * AUG_W + HEAD_DIM] = q_ref[:, sl]
        ka_ref[:, h * AUG_W:h * AUG_W + HEAD_DIM] = k_ref[:, sl]
        eq = jnp.dot(pieces, selq_ref[h], preferred_element_type=F32) + cq_ref[...]
        ek = jnp.dot(pieces, selk_ref[h], preferred_element_type=F32) + ck_ref[...]
        qa_ref[:, h * AUG_W + HEAD_DIM:(h + 1) * AUG_W] = eq.astype(BF16)
        ka_ref[:, h * AUG_W + HEAD_DIM:(h + 1) * AUG_W] = ek.astype(BF16)


def _aug_tables():
    selq = [[[0.0] * LANES for _ in range(LANES)] for _ in range(N_FOX_HEADS)]
    selk = [[[0.0] * LANES for _ in range(LANES)] for _ in range(N_FOX_HEADS)]
    for h in range(N_FOX_HEADS):
        for p in range(N_PIECES):
            selq[h][p * N_FOX_HEADS + h][p] = 1.0
            selk[h][p * N_FOX_HEADS + h][N_PIECES + p] = -1.0
    cq = [[1.0 if N_PIECES <= c < 2 * N_PIECES else 0.0 for c in range(LANES)]]
    ck = [[1.0 if c < N_PIECES else 0.0 for c in range(LANES)]]
    return (jnp.array(selq, BF16), jnp.array(selk, BF16), jnp.array(cq, F32), jnp.array(ck, F32))


def _augment(q, k, cum2, tm=512):
    m, w = q.shape
    selq, selk, cq, ck = _aug_tables()
    row = lambda i: (i, 0)
    const3 = lambda i: (0, 0, 0)
    const2 = lambda i: (0, 0)
    wa = N_FOX_HEADS * AUG_W
    return pl.pallas_call(
        _aug_kernel,
        out_shape=(jax.ShapeDtypeStruct((m, wa), BF16), jax.ShapeDtypeStruct((m, wa), BF16)),
        grid=(m // tm,),
        in_specs=[pl.BlockSpec((tm, w), row), pl.BlockSpec((tm, w), row), pl.BlockSpec((tm, LANES), row),
                  pl.BlockSpec((N_FOX_HEADS, LANES, LANES), const3), pl.BlockSpec((N_FOX_HEADS, LANES, LANES), const3),
                  pl.BlockSpec((1, LANES), const2), pl.BlockSpec((1, LANES), const2)],
        out_specs=(pl.BlockSpec((tm, wa), row), pl.BlockSpec((tm, wa), row)),
        compiler_params=_cparams(("parallel",)),
        name="augment",
    )(q, k, cum2, selq, selk, cq, ck)


ATT_TQ = 1024
ATT_TK = 1024


def _causal_pairs(seq, tq, tk):
    r = tq // tk
    pairs = [(i, j) for i in range(seq // tq) for j in range(r * (i + 1))]
    return (jnp.array([p[0] for p in pairs], jnp.int32), jnp.array([p[1] for p in pairs], jnp.int32))


def _fox_kernel(qi_ref, kj_ref, q_ref, k_ref, v_ref, o_ref, m_scr, l_scr, acc_ref):
    t = pl.program_id(1)
    i, j = qi_ref[t], kj_ref[t]
    tq, tk = q_ref.shape[0], k_ref.shape[0]
    r = tq // tk

    @pl.when(j == 0)
    def _():
        _attn_init(m_scr, l_scr, acc_ref)

    def tile(masked):
        if masked:
            col = lax.broadcasted_iota(jnp.int32, (tq, tk), 1) + (j * tk - i * tq)
            keep = col <= lax.broadcasted_iota(jnp.int32, (tq, tk), 0)
        for h in range(N_FOX_HEADS):
            s = _qk(q_ref, k_ref, h, AUG_W)
            if masked:
                s = jnp.where(keep, s, NEG_INF)
            _softmax_update(s, v_ref[:, h * HEAD_DIM:(h + 1) * HEAD_DIM], m_scr, l_scr, acc_ref, h)

    @pl.when(j < r * i)
    def _():
        tile(False)

    @pl.when(j >= r * i)
    def _():
        tile(True)

    @pl.when(j == r * i + r - 1)
    def _():
        _attn_finish(o_ref, l_scr, acc_ref, N_FOX_HEADS)


def _fox(qa, ka, v):
    b, seq, w = v.shape
    wa = qa.shape[2]
    tq, tk = min(ATT_TQ, seq), min(ATT_TK, seq)
    qi, kj = _causal_pairs(seq, tq, tk)
    q_map = lambda bb, t, qi, kj: (bb, qi[t], 0)
    kv_map = lambda bb, t, qi, kj: (bb, kj[t], 0)
    return pl.pallas_call(
        _fox_kernel,
        out_shape=jax.ShapeDtypeStruct((b, seq, w), BF16),
        grid_spec=pltpu.PrefetchScalarGridSpec(
            num_scalar_prefetch=2,
            grid=(b, qi.shape[0]),
            in_specs=[pl.BlockSpec((None, tq, wa), q_map),
                      pl.BlockSpec((None, tk, wa), kv_map),
                      pl.BlockSpec((None, tk, w), kv_map)],
            out_specs=pl.BlockSpec((None, tq, w), q_map),
            scratch_shapes=[pltpu.VMEM((N_FOX_HEADS, tq, LANES), F32),
                            pltpu.VMEM((N_FOX_HEADS, tq, LANES), F32),
                            pltpu.VMEM((tq, w), F32)]),
        compiler_params=_cparams(("parallel", "arbitrary")),
        name="fox",
    )(qi, kj, qa, ka, v)


IDX_ROWS = 128
IDX_CH = 256


def _index_kernel(qi_ref, misc_ref, kit_ref, o_ref, keys_scr, wb_scr, *, topk):
    i = pl.program_id(1)
    seq = kit_ref.shape[1]
    nsub = IDX_CH // LANES
    t0 = i * IDX_ROWS
    npair = (t0 + IDX_ROWS + 2 * IDX_CH - 1) // (2 * IDX_CH)

    def chunk_loop(fn, init):
        return lax.fori_loop(0, npair, lambda cp, carry: fn(2 * cp + 1, fn(2 * cp, carry)), init)
    w = misc_ref[:, MISC_WI:MISC_WI + N_IDX_HEADS] * (N_IDX_HEADS ** -0.5 * IDX_DIM ** -0.5)
    for jh in range(N_IDX_HEADS):
        wb_scr[jh] = jnp.broadcast_to(w[:, jh:jh + 1], (IDX_ROWS, LANES))
    row = t0 + lax.broadcasted_iota(jnp.int32, (IDX_ROWS, LANES), 0)
    lane = lax.broadcasted_iota(jnp.int32, (IDX_ROWS, LANES), 1)

    def score_chunk(c, carry):
        c0 = pl.multiple_of(c * IDX_CH, IDX_CH)
        kc = kit_ref[:, pl.ds(c0, IDX_CH)]
        z = jnp.zeros_like(kc)
        rhs = jnp.concatenate([jnp.concatenate([kc, z], axis=1), jnp.concatenate([z, kc], axis=1)], axis=0)
        accs = [jnp.zeros((IDX_ROWS, LANES), F32) for _ in range(nsub)]
        for p in range(N_IDX_HEADS // 2):
            s2 = jnp.dot(qi_ref[:, p * 2 * IDX_DIM:(p + 1) * 2 * IDX_DIM], rhs, preferred_element_type=F32)
            wa, wb = wb_scr[2 * p], wb_scr[2 * p + 1]
            for u in range(nsub):
                accs[u] = (accs[u] + wa * jnp.maximum(s2[:, u * LANES:(u + 1) * LANES], 0.0)
                           + wb * jnp.maximum(s2[:, IDX_CH + u * LANES:IDX_CH + (u + 1) * LANES], 0.0))
        for u in range(nsub):
            bits = pltpu.bitcast(accs[u], jnp.int32)
            key = bits ^ ((bits >> 31) & 0x7FFFFFFF)
            col = c0 + u * LANES + lane
            keys_scr[:, pl.ds(pl.multiple_of(c0 + u * LANES, LANES), LANES)] = jnp.where(col <= row, key, INT_MIN)
        return carry

    chunk_loop(score_chunk, 0)

    def count_ge(cand):
        def body(c, acc):
            c0 = pl.multiple_of(c * IDX_CH, IDX_CH)
            blk = keys_scr[:, pl.ds(c0, IDX_CH)]
            for u in range(nsub):
                acc = acc + jnp.where(blk[:, u * LANES:(u + 1) * LANES] >= cand, 1.0, 0.0)
            return acc
        acc = chunk_loop(body, jnp.zeros((IDX_ROWS, LANES), F32))
        return jnp.broadcast_to(jnp.sum(acc, axis=1, keepdims=True), (IDX_ROWS, LANES))

    def lane_all(x, op):
        s = LANES // 2
        while s >= 1:
            x = op(x, pltpu.roll(x, s, axis=1))
            s //= 2
        return x

    def gmax_body(c, gs):
        c0 = pl.multiple_of(c * IDX_CH, IDX_CH)
        blk = keys_scr[:, pl.ds(c0, IDX_CH)]
        return tuple(jnp.maximum(g, blk[:, u * LANES:(u + 1) * LANES]) for u, g in enumerate(gs))

    gs = chunk_loop(gmax_body, tuple(jnp.full((IDX_ROWS, LANES), INT_MIN, jnp.int32) for _ in range(nsub)))
    gmin, gmax = gs[0], gs[0]
    for g in gs[1:]:
        gmin, gmax = jnp.minimum(gmin, g), jnp.maximum(gmax, g)
    short = row < topk

    def all_true(flag):
        return (jnp.min(jnp.where(flag, 1.0, 0.0)) > 0.5).astype(jnp.int32)

    def settled(lo, hi, clo):
        return all_true(short | (clo == topk) | (hi - lo == 1))

    def bisect(state):
        it, lo, hi, clo, chi, _ = state
        mid = (lo >> 1) + (hi >> 1) + (lo & hi & 1)
        cnt = count_ge(mid)
        ge = cnt >= topk
        lo, hi = jnp.where(ge, mid, lo), jnp.where(ge, hi, mid)
        clo, chi = jnp.where(ge, cnt, clo), jnp.where(ge, chi, cnt)
        return it + 1, lo, hi, clo, chi, settled(lo, hi, clo)

    lo0 = lane_all(gmin, jnp.minimum)
    hi0 = lane_all(gmax, jnp.maximum) + 1
    clo0 = jnp.full((IDX_ROWS, LANES), -1.0, F32)
    chi0 = jnp.zeros((IDX_ROWS, LANES), F32)
    _, lo, _, clo, chi, _ = lax.while_loop(lambda st: (st[5] == 0) & (st[0] < 34), bisect,
                                           (jnp.int32(0), lo0, hi0, clo0, chi0, settled(lo0, hi0, clo0)))
    thr = jnp.where(short, INT_MIN + 1, jnp.maximum(lo, INT_MIN + 1))
    tied = jnp.logical_not(short) & (clo != topk)
    no_ties = all_true(jnp.logical_not(tied))

    def emit_chunks(select):
        def emit(c, carry):
            c0 = pl.multiple_of(c * IDX_CH, IDX_CH)
            blk = keys_scr[:, pl.ds(c0, IDX_CH)]
            for u in range(nsub):
                sel = select(blk[:, u * LANES:(u + 1) * LANES], c0 + u * LANES + lane)
                o_ref[:, pl.ds(pl.multiple_of(c0 + u * LANES, LANES), LANES)] = jnp.where(sel, 0.0, NEG_INF).astype(o_ref.dtype)
            return carry
        chunk_loop(emit, 0)

    @pl.when(no_ties == 1)
    def _():
        emit_chunks(lambda key, col: key >= thr)

    @pl.when(no_ties == 0)
    def _():
        need = topk - chi

        def count_tied_upto(col_max):
            def body(c, acc):
                c0 = pl.multiple_of(c * IDX_CH, IDX_CH)
                blk = keys_scr[:, pl.ds(c0, IDX_CH)]
                for u in range(nsub):
                    hit = (blk[:, u * LANES:(u + 1) * LANES] == lo) & (c0 + u * LANES + lane <= col_max)
                    acc = acc + jnp.where(hit, 1.0, 0.0)
                return acc
            acc = chunk_loop(body, jnp.zeros((IDX_ROWS, LANES), F32))
            return jnp.broadcast_to(jnp.sum(acc, axis=1, keepdims=True), (IDX_ROWS, LANES))

        def col_step(_, state):
            below, last = state
            mid = (below + last) >> 1
            enough = count_tied_upto(mid) >= need
            return jnp.where(enough, below, mid), jnp.where(enough, mid, last)

        steps = max(seq - 1, 1).bit_length() + 1
        _, last = lax.fori_loop(0, steps, col_step, (jnp.full((IDX_ROWS, LANES), -1, jnp.int32),
                                                      jnp.full((IDX_ROWS, LANES), seq - 1, jnp.int32)))
        untied = jnp.logical_not(tied)
        emit_chunks(lambda key, col: (tied & ((key > lo) | ((key == lo) & (col <= last)))) | (untied & (key >= thr)))

    def fill(c, carry):
        c0 = pl.multiple_of(c * IDX_CH, IDX_CH)
        o_ref[:, pl.ds(c0, IDX_CH)] = jnp.full((IDX_ROWS, IDX_CH), NEG_INF, o_ref.dtype)
        return carry

    lax.fori_loop(2 * npair, seq // IDX_CH, fill, 0)


def _index(qi, misc3, kit, topk):
    b, seq, w = qi.shape
    return pl.pallas_call(
        functools.partial(_index_kernel, topk=topk),
        out_shape=jax.ShapeDtypeStruct((b, seq, seq), BF16),
        grid=(b, seq // IDX_ROWS),
        in_specs=[pl.BlockSpec((None, IDX_ROWS, w), lambda bb, i: (bb, i, 0)),
                  pl.BlockSpec((None, IDX_ROWS, LANES), lambda bb, i: (bb, i, 0)),
                  pl.BlockSpec((None, IDX_DIM, seq), lambda bb, i: (bb, 0, 0))],
        out_specs=pl.BlockSpec((None, IDX_ROWS, seq), lambda bb, i: (bb, i, 0)),
        scratch_shapes=[pltpu.VMEM((IDX_ROWS, seq), jnp.int32),
                        pltpu.VMEM((N_IDX_HEADS, IDX_ROWS, LANES), F32)],
        compiler_params=_cparams(("parallel", "arbitrary")),
        name="index",
    )(qi, misc3, kit)


def _dsa_kernel(qi_ref, kj_ref, tab_ref, q_ref, k_ref, v_ref, b_ref, o_ref, m_scr, l_scr, acc_ref, toe_scr, s_scr):
    t = pl.program_id(1)
    i, j = qi_ref[t], kj_ref[t]
    tq, tk = q_ref.shape[0], k_ref.shape[0]
    r = tq // tk
    nbq, nbk = tq // LANES, tk // LANES

    @pl.when(j == 0)
    def _():
        _attn_init(m_scr, l_scr, acc_ref)

    @pl.when(t == 0)
    def _():
        row = lax.broadcasted_iota(jnp.int32, (LANES, LANES), 0)
        col = lax.broadcasted_iota(jnp.int32, (LANES, LANES), 1)
        d_diag, d_sub = row - col, LANES + row - col
        for h in range(N_DSA_HEADS):
            def body(d, carry, h=h):
                td, ts = carry
                val = tab_ref[h, d]
                return jnp.where(d_diag == d, val, td), jnp.where(d_sub == d, val, ts)
            zero = jnp.zeros((LANES, LANES), F32)
            td, ts = lax.fori_loop(0, LANES, body, (zero, zero))
            toe_scr[h, 0] = td
            toe_scr[h, 1] = ts

    def tile(e):
        near = [] if e is None else [(a, c, a - c - e * nbk) for a in range(nbq) for c in range(nbk)
                                     if a - c - e * nbk in (0, 1)]
        bias = b_ref[...].astype(F32)
        for h in range(N_DSA_HEADS):
            s = _qk(q_ref, k_ref, h) + bias
            if near:
                s_scr[...] = s
                for a, c, which in near:
                    s_scr[a * LANES:(a + 1) * LANES, c * LANES:(c + 1) * LANES] += toe_scr[h, which]
                s = s_scr[...]
            _softmax_update(s, v_ref[:, h * HEAD_DIM:(h + 1) * HEAD_DIM], m_scr, l_scr, acc_ref, h)

    @pl.when(j < r * i - 1)
    def _():
        tile(None)

    for e in range(-1, r):
        @pl.when(j == r * i + e)
        def _(e=e):
            tile(e)

    @pl.when(j == r * i + r - 1)
    def _():
        _attn_finish(o_ref, l_scr, acc_ref, N_DSA_HEADS)


def _dsa(tab, q, k, v, bias):
    b, seq, w = q.shape
    tq, tk = min(ATT_TQ, seq), min(ATT_TK, seq)
    qi, kj = _causal_pairs(seq, tq, tk)
    q_map = lambda bb, t, qi, kj: (bb, qi[t], 0)
    kv_map = lambda bb, t, qi, kj: (bb, kj[t], 0)
    return pl.pallas_call(
        _dsa_kernel,
        out_shape=jax.ShapeDtypeStruct((b, seq, w), BF16),
        grid_spec=pltpu.PrefetchScalarGridSpec(
            num_scalar_prefetch=2,
            grid=(b, qi.shape[0]),
            in_specs=[pl.BlockSpec(memory_space=pltpu.SMEM),
                      pl.BlockSpec((None, tq, w), q_map),
                      pl.BlockSpec((None, tk, w), kv_map),
                      pl.BlockSpec((None, tk, w), kv_map),
                      pl.BlockSpec((None, tq, tk), lambda bb, t, qi, kj: (bb, qi[t], kj[t]))],
            out_specs=pl.BlockSpec((None, tq, w), q_map),
            scratch_shapes=[pltpu.VMEM((N_DSA_HEADS, tq, LANES), F32),
                            pltpu.VMEM((N_DSA_HEADS, tq, LANES), F32),
                            pltpu.VMEM((tq, w), F32),
                            pltpu.VMEM((N_DSA_HEADS, 2, LANES, LANES), F32),
                            pltpu.VMEM((tq, tk), F32)]),
        compiler_params=_cparams(("arbitrary", "arbitrary")),
        name="dsa",
    )(qi, kj, tab, q, k, v, bias)


def _merge_kernel(af_ref, ad_ref, ga_ref, gb_ref, x_ref, mod_ref, wof_ref, wod_ref, wo_ref, g2_ref,
                  x1_ref, h2_ref):
    yf = jnp.dot(af_ref[...], wof_ref[...], preferred_element_type=F32)
    yd = jnp.dot(ad_ref[...], wod_ref[...], preferred_element_type=F32)
    merged = ga_ref[...].astype(F32) * yf + gb_ref[...].astype(F32) * yd
    o = jnp.dot(merged.astype(BF16), wo_ref[...], preferred_element_type=F32)
    x1 = x_ref[...] + mod_ref[2:3, :] * o
    x1_ref[...] = x1
    h2_ref[...] = _modulated_norm(x1, g2_ref[...], mod_ref[4:5, :], mod_ref[3:4, :]).astype(h2_ref.dtype)


def _merge(af, ad, ga, gb, x2, mod3, wof, wod, wo, g2, seq, tm=256):
    m, d = x2.shape
    per_b = seq // tm
    row = lambda i: (i, 0)
    const = lambda i: (0, 0)
    return pl.pallas_call(
        _merge_kernel,
        out_shape=(jax.ShapeDtypeStruct((m, d), F32), jax.ShapeDtypeStruct((m, d), BF16)),
        grid=(m // tm,),
        in_specs=[pl.BlockSpec((tm, FOX_W), row), pl.BlockSpec((tm, DSA_W), row),
                  pl.BlockSpec((tm, d), row), pl.BlockSpec((tm, d), lambda i: (i, 1)), pl.BlockSpec((tm, d), row),
                  pl.BlockSpec((None, 6, d), lambda i: (i // per_b, 0, 0)),
                  pl.BlockSpec((FOX_W, d), const), pl.BlockSpec((DSA_W, d), const), pl.BlockSpec((d, d), const),
                  pl.BlockSpec((1, d), const)],
        out_specs=(pl.BlockSpec((tm, d), row), pl.BlockSpec((tm, d), row)),
        compiler_params=_cparams(("parallel",)),
        name="merge",
    )(af, ad, ga, gb, x2, mod3, wof, wod, wo, g2)


FFN_HALO = 16


def _ffn_kernel(h_ref, halo_ref, wa_ref, wb_ref, cwa_ref, cwb_ref, cba_ref, cbb_ref, wout_ref, x1_ref, mod_ref,
                o_ref, hext_scr, acc_ref, *, per_b):
    i = pl.program_id(0)
    f = pl.program_id(1)
    tm = h_ref.shape[0]

    @pl.when(f == 0)
    def _():
        first = (i % per_b) == 0
        hext_scr[0:FFN_HALO, :] = jnp.where(first, jnp.zeros_like(halo_ref[...]), halo_ref[...])
        hext_scr[FFN_HALO:, :] = h_ref[...]
        acc_ref[...] = jnp.zeros_like(acc_ref)

    hext = hext_scr[...]

    def conv(w_ref, cw_ref, cb_ref):
        u = jnp.dot(hext, w_ref[...], preferred_element_type=F32)
        y = cw_ref[2:3, :] * u + cw_ref[1:2, :] * pltpu.roll(u, 1, axis=0) + cw_ref[0:1, :] * pltpu.roll(u, 2, axis=0)
        return y[FFN_HALO:, :] + cb_ref[...]

    ya = conv(wa_ref, cwa_ref, cba_ref)
    yb = conv(wb_ref, cwb_ref, cbb_ref)
    act = (ya * jax.nn.sigmoid(ya) * yb).astype(BF16)
    acc_ref[...] += jnp.dot(act, wout_ref[...], preferred_element_type=F32)

    @pl.when(f == pl.num_programs(1) - 1)
    def _():
        o_ref[...] = x1_ref[...] + mod_ref[5:6, :] * acc_ref[...]


def _ffn(h2, w_in, conv_w, conv_b, w_out, x1, mod3, seq, tm=512, tf=512):
    m, d = h2.shape
    dff = w_out.shape[0]
    nf = dff // tf
    per_b = seq // tm
    hb = tm // FFN_HALO
    return pl.pallas_call(
        functools.partial(_ffn_kernel, per_b=per_b),
        out_shape=jax.ShapeDtypeStruct((m, d), F32),
        grid=(m // tm, nf),
        in_specs=[pl.BlockSpec((tm, d), lambda i, f: (i, 0)),
                  pl.BlockSpec((FFN_HALO, d), lambda i, f: (jnp.maximum(i * hb - 1, 0), 0)),
                  pl.BlockSpec((d, tf), lambda i, f: (0, f)),
                  pl.BlockSpec((d, tf), lambda i, f: (0, f + nf)),
                  pl.BlockSpec((CONV_WIDTH, tf), lambda i, f: (0, f)),
                  pl.BlockSpec((CONV_WIDTH, tf), lambda i, f: (0, f + nf)),
                  pl.BlockSpec((1, tf), lambda i, f: (0, f)),
                  pl.BlockSpec((1, tf), lambda i, f: (0, f + nf)),
                  pl.BlockSpec((tf, d), lambda i, f: (f, 0)),
                  pl.BlockSpec((tm, d), lambda i, f: (i, 0)),
                  pl.BlockSpec((None, 6, d), lambda i, f: (i // per_b, 0, 0))],
        out_specs=pl.BlockSpec((tm, d), lambda i, f: (i, 0)),
        scratch_shapes=[pltpu.VMEM((tm + FFN_HALO, d), BF16), pltpu.VMEM((tm, d), F32)],
        compiler_params=_cparams(("parallel", "arbitrary")),
        name="ffn",
    )(h2, h2, w_in, w_in, conv_w, conv_w, conv_b, conv_b, w_out, x1, mod3)


def _t5_bucket(n):
    n = jnp.maximum(n, 0)
    max_exact = N_BUCKETS // 2
    nf = jnp.maximum(n, 1).astype(F32)
    large = max_exact + (jnp.log(nf / max_exact) / math.log(MAX_DISTANCE / max_exact)
                         * (N_BUCKETS - max_exact)).astype(jnp.int32)
    large = jnp.minimum(large, N_BUCKETS - 1)
    return jnp.where(n < max_exact, n, large)


def _layer(x, c8, w_ada, b_ada, norm1_g, w_in, b_forget, q_norm_fox, k_norm_fox, kv_norm_g, w_ukv, q_norm_dsa,
           k_norm_dsa, w_out_fox, w_out_dsa, w_out, norm2_g, w_ffn_in, conv_w, conv_b, w_ffn_out, rel_bias):
    b, seq, d = x.shape
    m = b * seq
    topk = min(TOPK_MAX, seq // 4)
    x2 = x.reshape(m, d)

    mod3 = _ada(c8, w_ada, b_ada.reshape(1, -1))[:b].reshape(b, 6, d)
    h1 = _normmod(x2, norm1_g.reshape(1, d), mod3, seq)

    o = 0
    cols = {}
    for name, size in (("qf", FOX_W), ("kf", FOX_W), ("vf", FOX_W), ("fg", N_FOX_HEADS), ("qd", DSA_W),
                       ("ckv", KV_LORA), ("qi", N_IDX_HEADS * IDX_DIM), ("ki", IDX_DIM), ("wi", N_IDX_HEADS),
                       ("ga", d), ("gb", d)):
        cols[name] = w_in[:, o:o + size]
        o += size
    wb = lambda a: a.astype(BF16)
    scale = HEAD_DIM ** -0.5 * LOG2E
    tile_h = lambda g, nh: jnp.tile(g.reshape(1, HEAD_DIM), (1, nh))

    qf = _proj(h1, wb(cols["qf"]), tile_h(q_norm_fox, N_FOX_HEADS) * scale, mode="headnorm", out_dtype=BF16, name="proj_qf")
    kf = _proj(h1, wb(cols["kf"]), tile_h(k_norm_fox, N_FOX_HEADS), mode="headnorm", out_dtype=BF16, name="proj_kf")
    vf = _proj(h1, wb(cols["vf"]), None, mode="plain", out_dtype=BF16, name="proj_vf")
    qd = _proj(h1, wb(cols["qd"]), tile_h(q_norm_dsa, N_DSA_HEADS) * scale, mode="headnorm", out_dtype=BF16, name="proj_qd")
    qi = _proj(h1, wb(cols["qi"]), None, mode="plain", out_dtype=BF16, name="proj_qi")
    gates = _proj(h1, wb(jnp.concatenate([cols["ga"], cols["gb"]], axis=1)), None, mode="sigmoid", out_dtype=BF16,
                  name="proj_gates")
    pad = jnp.zeros((d, LANES - N_FOX_HEADS - N_IDX_HEADS - IDX_DIM), w_in.dtype)
    misc = _proj(h1, wb(jnp.concatenate([cols["fg"], cols["wi"], cols["ki"], pad], axis=1)), None, mode="plain",
                 out_dtype=F32, name="proj_misc")
    kd, vd = _ckv(h1, wb(cols["ckv"]), kv_norm_g.reshape(1, KV_LORA), wb(w_ukv), tile_h(k_norm_dsa, N_DSA_HEADS))

    misc3 = misc.reshape(b, seq, LANES)
    bf = jnp.zeros((1, LANES), F32).at[0, :N_FOX_HEADS].set(b_forget.astype(F32))
    cum = _cumsum(misc3, bf)
    r3 = lambda a: a.reshape(b, seq, -1)
    qa, ka = _augment(qf, kf, cum.reshape(m, LANES))
    a_fox = _fox(r3(qa), r3(ka), r3(vf))

    kit = jnp.swapaxes(misc3[:, :, MISC_KI:MISC_KI + IDX_DIM], 1, 2).astype(BF16)
    sel_bias = _index(r3(qi), misc3, kit, topk)
    by_dist = rel_bias[_t5_bucket(jnp.arange(LANES, dtype=jnp.int32))] - rel_bias[N_BUCKETS - 1][None, :]
    a_dsa = _dsa((by_dist.T * LOG2E).astype(F32), r3(qd), r3(kd), r3(vd), sel_bias)

    x1, h2 = _merge(a_fox.reshape(m, FOX_W), a_dsa.reshape(m, DSA_W), gates, gates, x2, mod3,
                    wb(w_out_fox), wb(w_out_dsa), wb(w_out), norm2_g.reshape(1, d), seq)
    out = _ffn(h2, wb(w_ffn_in), conv_w, conv_b.reshape(1, -1), wb(w_ffn_out), x1, mod3, seq)
    return out.reshape(b, seq, d)


def kernel(x, c, w_ada, b_ada, norm1_g, w_in, b_forget, q_norm_fox, k_norm_fox, kv_norm_g, w_ukv, q_norm_dsa, k_norm_dsa, w_out_fox, w_out_dsa, w_out, norm2_g, w_ffn_in, conv_w, conv_b, w_ffn_out, rel_bias):
    b = x.shape[0]
    c8 = jnp.zeros((8, c.shape[1]), c.dtype).at[:b].set(c)
    for l in range(w_ada.shape[0]):
        x = _layer(x, c8, w_ada[l], b_ada[l], norm1_g[l], w_in[l], b_forget[l], q_norm_fox[l], k_norm_fox[l],
                   kv_norm_g[l], w_ukv[l], q_norm_dsa[l], k_norm_dsa[l], w_out_fox[l], w_out_dsa[l], w_out[l],
                   norm2_g[l], w_ffn_in[l], conv_w[l], conv_b[l], w_ffn_out[l], rel_bias)
    return x
```

```python
import functools
import math

import jax
import jax.numpy as jnp
from jax import lax
from jax.experimental import pallas as pl
from jax.experimental.pallas import tpu as pltpu

HEAD_DIM = 128
N_FOX_HEADS = 8
N_DSA_HEADS = 8
FOX_W = N_FOX_HEADS * HEAD_DIM
DSA_W = N_DSA_HEADS * HEAD_DIM
KV_LORA = 256
N_IDX_HEADS = 16
IDX_DIM = 64
TOPK_MAX = 256
N_BUCKETS = 32
MAX_DISTANCE = 128
CONV_WIDTH = 3
EPS = 1e-6
NEG_INF = -1e30
LOG2E = 1.4426950408889634

LANES = 128
INT_MIN = -(2 ** 31)
VMEM_LIMIT = 56 * 1024 * 1024

F32 = jnp.float32
BF16 = jnp.bfloat16

MISC_FG = 0
MISC_WI = N_FOX_HEADS
MISC_KI = MISC_WI + N_IDX_HEADS


def _cparams(sem):
    return pltpu.CompilerParams(dimension_semantics=sem, vmem_limit_bytes=VMEM_LIMIT)


def _ada_kernel(c_ref, w_ref, b_ref, o_ref):
    c = c_ref[...]
    ca = c * jax.nn.sigmoid(c)
    o_ref[...] = jnp.dot(ca, w_ref[...], preferred_element_type=F32,
                         precision=lax.Precision.HIGHEST) + b_ref[...]


def _ada(c8, w, b, tn=1024):
    rows, d = c8.shape
    n = w.shape[1]
    return pl.pallas_call(
        _ada_kernel,
        out_shape=jax.ShapeDtypeStruct((rows, n), F32),
        grid=(n // tn,),
        in_specs=[pl.BlockSpec((rows, d), lambda j: (0, 0)),
                  pl.BlockSpec((d, tn), lambda j: (0, j)),
                  pl.BlockSpec((1, tn), lambda j: (0, j))],
        out_specs=pl.BlockSpec((rows, tn), lambda j: (0, j)),
        compiler_params=_cparams(("arbitrary",)),
        name="ada",
    )(c8, w, b)


def _modulated_norm(x, g, sc, sh):
    ms = jnp.mean(x * x, axis=-1, keepdims=True)
    return (x * lax.rsqrt(ms + EPS) * g) * (1.0 + sc) + sh


def _normmod_kernel(x_ref, g_ref, mod_ref, o_ref):
    o_ref[...] = _modulated_norm(x_ref[...], g_ref[...], mod_ref[1:2, :], mod_ref[0:1, :]).astype(o_ref.dtype)


def _normmod(x2, g, mod3, seq, tm=512):
    m, d = x2.shape
    per_b = seq // tm
    return pl.pallas_call(
        _normmod_kernel,
        out_shape=jax.ShapeDtypeStruct((m, d), BF16),
        grid=(m // tm,),
        in_specs=[pl.BlockSpec((tm, d), lambda i: (i, 0)),
                  pl.BlockSpec((1, d), lambda i: (0, 0)),
                  pl.BlockSpec((None, 6, d), lambda i: (i // per_b, 0, 0))],
        out_specs=pl.BlockSpec((tm, d), lambda i: (i, 0)),
        compiler_params=_cparams(("parallel",)),
        name="normmod",
    )(x2, g, mod3)


def _head_norm_store(acc, gain_ref, o_ref, col0=0):
    for hh in range(acc.shape[1] // HEAD_DIM):
        a = acc[:, hh * HEAD_DIM:(hh + 1) * HEAD_DIM]
        ms = jnp.mean(a * a, axis=-1, keepdims=True)
        sl = slice(col0 + hh * HEAD_DIM, col0 + (hh + 1) * HEAD_DIM)
        o_ref[:, sl] = (a * lax.rsqrt(ms + EPS) * gain_ref[:, hh * HEAD_DIM:(hh + 1) * HEAD_DIM]).astype(o_ref.dtype)


def _proj_kernel(h_ref, w_ref, gain_ref, o_ref, *, mode):
    acc = jnp.dot(h_ref[...], w_ref[...], preferred_element_type=F32)
    if mode == "headnorm":
        _head_norm_store(acc, gain_ref, o_ref)
    elif mode == "sigmoid":
        o_ref[...] = jax.nn.sigmoid(acc).astype(o_ref.dtype)
    else:
        o_ref[...] = acc.astype(o_ref.dtype)


def _proj(h, w, gain, *, mode, out_dtype, tm=512, tn=1024, name="proj"):
    m, d = h.shape
    n = w.shape[1]
    tn = min(tn, n)
    if gain is None:
        gain = jnp.ones((1, n), F32)
    return pl.pallas_call(
        functools.partial(_proj_kernel, mode=mode),
        out_shape=jax.ShapeDtypeStruct((m, n), out_dtype),
        grid=(n // tn, m // tm),
        in_specs=[pl.BlockSpec((tm, d), lambda j, i: (i, 0)),
                  pl.BlockSpec((d, tn), lambda j, i: (0, j)),
                  pl.BlockSpec((1, tn), lambda j, i: (0, j))],
        out_specs=pl.BlockSpec((tm, tn), lambda j, i: (i, j)),
        compiler_params=_cparams(("parallel", "parallel")),
        name=name,
    )(h, w, gain)


def _ckv_kernel(h_ref, wc_ref, g_ref, wu_ref, gk_ref, k_ref, v_ref):
    c = jnp.dot(h_ref[...], wc_ref[...], preferred_element_type=F32)
    ms = jnp.mean(c * c, axis=-1, keepdims=True)
    cn = (c * lax.rsqrt(ms + EPS) * g_ref[...]).astype(BF16)
    kv = jnp.dot(cn, wu_ref[...], preferred_element_type=F32)
    _head_norm_store(kv[:, :DSA_W], gk_ref, k_ref)
    v_ref[...] = kv[:, DSA_W:].astype(v_ref.dtype)


def _ckv(h, wc, g, wu, gk, tm=512):
    m, d = h.shape
    return pl.pallas_call(
        _ckv_kernel,
        out_shape=(jax.ShapeDtypeStruct((m, DSA_W), BF16), jax.ShapeDtypeStruct((m, DSA_W), BF16)),
        grid=(m // tm,),
        in_specs=[pl.BlockSpec((tm, d), lambda i: (i, 0)),
                  pl.BlockSpec((d, KV_LORA), lambda i: (0, 0)),
                  pl.BlockSpec((1, KV_LORA), lambda i: (0, 0)),
                  pl.BlockSpec((KV_LORA, 2 * DSA_W), lambda i: (0, 0)),
                  pl.BlockSpec((1, DSA_W), lambda i: (0, 0))],
        out_specs=(pl.BlockSpec((tm, DSA_W), lambda i: (i, 0)), pl.BlockSpec((tm, DSA_W), lambda i: (i, 0))),
        compiler_params=_cparams(("parallel",)),
        name="ckv",
    )(h, wc, g, wu, gk)


def _cum_kernel(m_ref, bf_ref, o_ref, carry_ref):
    @pl.when(pl.program_id(1) == 0)
    def _():
        carry_ref[...] = jnp.zeros_like(carry_ref)

    z = m_ref[...] + bf_ref[...]
    lf = -(jnp.maximum(-z, 0.0) + jnp.log1p(jnp.exp(-jnp.abs(z))))
    tc = lf.shape[0]
    row = lax.broadcasted_iota(jnp.int32, lf.shape, 0)
    s = 1
    while s < tc:
        lf = lf + jnp.where(row >= s, pltpu.roll(lf, s, axis=0), 0.0)
        s *= 2
    out = lf + carry_ref[0:1, :]
    o_ref[...] = out
    carry_ref[...] = jnp.broadcast_to(out[tc - 1:tc, :], carry_ref.shape)


def _cumsum(misc3, bf, tc=512):
    b, seq, w = misc3.shape
    return pl.pallas_call(
        _cum_kernel,
        out_shape=jax.ShapeDtypeStruct((b, seq, w), F32),
        grid=(b, seq // tc),
        in_specs=[pl.BlockSpec((None, tc, w), lambda bb, i: (bb, i, 0)),
                  pl.BlockSpec((1, w), lambda bb, i: (0, 0))],
        out_specs=pl.BlockSpec((None, tc, w), lambda bb, i: (bb, i, 0)),
        scratch_shapes=[pltpu.VMEM((8, w), F32)],
        compiler_params=_cparams(("arbitrary", "arbitrary")),
        name="cumsum",
    )(misc3, bf)


def _softmax_update(s, v_h, m_scr, l_scr, acc_ref, h):
    m_prev = m_scr[h]
    m_new = jnp.maximum(m_prev, jnp.max(s, axis=1, keepdims=True))
    alpha = jnp.exp2(m_prev - m_new)
    p = jnp.exp2(s - m_new[:, :1]).astype(BF16)
    v_ones = jnp.concatenate([v_h, jnp.ones_like(v_h)], axis=1)
    pv = jnp.dot(p, v_ones, preferred_element_type=F32)
    l_scr[h] = alpha * l_scr[h] + pv[:, HEAD_DIM:]
    m_scr[h] = m_new
    sl = slice(h * HEAD_DIM, (h + 1) * HEAD_DIM)
    acc_ref[:, sl] = acc_ref[:, sl] * alpha + pv[:, :HEAD_DIM]


def _attn_init(m_scr, l_scr, acc_ref):
    m_scr[...] = jnp.full(m_scr.shape, NEG_INF, F32)
    l_scr[...] = jnp.zeros(l_scr.shape, F32)
    acc_ref[...] = jnp.zeros(acc_ref.shape, F32)


def _attn_finish(o_ref, l_scr, acc_ref, nheads):
    for h in range(nheads):
        sl = slice(h * HEAD_DIM, (h + 1) * HEAD_DIM)
        o_ref[:, sl] = (acc_ref[:, sl] / l_scr[h]).astype(o_ref.dtype)


def _qk(q_ref, k_ref, h, width=HEAD_DIM):
    sl = slice(h * width, (h + 1) * width)
    return lax.dot_general(q_ref[:, sl], k_ref[:, sl], (((1,), (1,)), ((), ())), preferred_element_type=F32)


AUG_W = 2 * HEAD_DIM
N_PIECES = 3


def _aug_kernel(q_ref, k_ref, cum_ref, selq_ref, selk_ref, cq_ref, ck_ref, qa_ref, ka_ref):
    lane = lax.broadcasted_iota(jnp.int32, cum_ref.shape, 1)
    c = jnp.where(lane < N_FOX_HEADS, cum_ref[...] * LOG2E, 0.0)
    hi = c.astype(BF16).astype(F32)
    r1 = c - hi
    mid = r1.astype(BF16).astype(F32)
    lo = (r1 - mid).astype(BF16).astype(F32)
    pieces = (hi + pltpu.roll(mid, N_FOX_HEADS, axis=1) + pltpu.roll(lo, 2 * N_FOX_HEADS, axis=1)).astype(BF16)
    for h in range(N_FOX_HEADS):
        sl = slice(h * HEAD_DIM, (h + 1) * HEAD_DIM)
        qa_ref[:, h * AUG_W:h * AUG_W + HEAD_DIM] = q_ref[:, sl]
        ka_ref[:, h * AUG_W:h * AUG_W + HEAD_DIM] = k_ref[:, sl]
        eq = jnp.dot(pieces, selq_ref[h], preferred_element_type=F32) + cq_ref[...]
        ek = jnp.dot(pieces, selk_ref[h], preferred_element_type=F32) + ck_ref[...]
        qa_ref[:, h * AUG_W + HEAD_DIM:(h + 1) * AUG_W] = eq.astype(BF16)
        ka_ref[:, h * AUG_W + HEAD_DIM:(h + 1) * AUG_W] = ek.astype(BF16)


def _aug_tables():
    selq = [[[0.0] * LANES for _ in range(LANES)] for _ in range(N_FOX_HEADS)]
    selk = [[[0.0] * LANES for _ in range(LANES)] for _ in range(N_FOX_HEADS)]
    for h in range(N_FOX_HEADS):
        for p in range(N_PIECES):
            selq[h][p * N_FOX_HEADS + h][p] = 1.0
            selk[h][p * N_FOX_HEADS + h][N_PIECES + p] = -1.0
    cq = [[1.0 if N_PIECES <= c < 2 * N_PIECES else 0.0 for c in range(LANES)]]
    ck = [[1.0 if c < N_PIECES else 0.0 for c in range(LANES)]]
    return (jnp.array(selq, BF16), jnp.array(selk, BF16), jnp.array(cq, F32), jnp.array(ck, F32))


def _augment(q, k, cum2, tm=512):
    m, w = q.shape
    selq, selk, cq, ck = _aug_tables()
    row = lambda i: (i, 0)
    const3 = lambda i: (0, 0, 0)
    const2 = lambda i: (0, 0)
    wa = N_FOX_HEADS * AUG_W
    return pl.pallas_call(
        _aug_kernel,
        out_shape=(jax.ShapeDtypeStruct((m, wa), BF16), jax.ShapeDtypeStruct((m, wa), BF16)),
        grid=(m // tm,),
        in_specs=[pl.BlockSpec((tm, w), row), pl.BlockSpec((tm, w), row), pl.BlockSpec((tm, LANES), row),
                  pl.BlockSpec((N_FOX_HEADS, LANES, LANES), const3), pl.BlockSpec((N_FOX_HEADS, LANES, LANES), const3),
                  pl.BlockSpec((1, LANES), const2), pl.BlockSpec((1, LANES), const2)],
        out_specs=(pl.BlockSpec((tm, wa), row), pl.BlockSpec((tm, wa), row)),
        compiler_params=_cparams(("parallel",)),
        name="augment",
    )(q, k, cum2, selq, selk, cq, ck)


ATT_TQ = 1024
FOX_TK = 1024
DSA_TK = 512


def _causal_pairs(seq, tq, tk):
    r = tq // tk
    pairs = [(i, j) for i in range(seq // tq) for j in range(r * (i + 1))]
    return (jnp.array([p[0] for p in pairs], jnp.int32), jnp.array([p[1] for p in pairs], jnp.int32))


def _fox_kernel(qi_ref, kj_ref, q_ref, k_ref, v_ref, o_ref, m_scr, l_scr, acc_ref):
    t = pl.program_id(1)
    i, j = qi_ref[t], kj_ref[t]
    tq, tk = q_ref.shape[0], k_ref.shape[0]
    r = tq // tk

    @pl.when(j == 0)
    def _():
        _attn_init(m_scr, l_scr, acc_ref)

    def tile(masked):
        if masked:
            col = lax.broadcasted_iota(jnp.int32, (tq, tk), 1) + (j * tk - i * tq)
            keep = col <= lax.broadcasted_iota(jnp.int32, (tq, tk), 0)
        for h in range(N_FOX_HEADS):
            s = _qk(q_ref, k_ref, h, AUG_W)
            if masked:
                s = jnp.where(keep, s, NEG_INF)
            _softmax_update(s, v_ref[:, h * HEAD_DIM:(h + 1) * HEAD_DIM], m_scr, l_scr, acc_ref, h)

    @pl.when(j < r * i)
    def _():
        tile(False)

    @pl.when(j >= r * i)
    def _():
        tile(True)

    @pl.when(j == r * i + r - 1)
    def _():
        _attn_finish(o_ref, l_scr, acc_ref, N_FOX_HEADS)


def _fox(qa, ka, v):
    b, seq, w = v.shape
    wa = qa.shape[2]
    tq, tk = min(ATT_TQ, seq), min(FOX_TK, seq)
    qi, kj = _causal_pairs(seq, tq, tk)
    q_map = lambda bb, t, qi, kj: (bb, qi[t], 0)
    kv_map = lambda bb, t, qi, kj: (bb, kj[t], 0)
    return pl.pallas_call(
        _fox_kernel,
        out_shape=jax.ShapeDtypeStruct((b, seq, w), BF16),
        grid_spec=pltpu.PrefetchScalarGridSpec(
            num_scalar_prefetch=2,
            grid=(b, qi.shape[0]),
            in_specs=[pl.BlockSpec((None, tq, wa), q_map),
                      pl.BlockSpec((None, tk, wa), kv_map),
                      pl.BlockSpec((None, tk, w), kv_map)],
            out_specs=pl.BlockSpec((None, tq, w), q_map),
            scratch_shapes=[pltpu.VMEM((N_FOX_HEADS, tq, LANES), F32),
                            pltpu.VMEM((N_FOX_HEADS, tq, LANES), F32),
                            pltpu.VMEM((tq, w), F32)]),
        compiler_params=_cparams(("parallel", "arbitrary")),
        name="fox",
    )(qi, kj, qa, ka, v)


IDX_ROWS = 128
IDX_CH = 256


def _index_kernel(qi_ref, misc_ref, kit_ref, o_ref, keys_scr, wb_scr, *, topk):
    i = pl.program_id(1)
    seq = kit_ref.shape[1]
    nsub = IDX_CH // LANES
    t0 = i * IDX_ROWS
    npair = (t0 + IDX_ROWS + 2 * IDX_CH - 1) // (2 * IDX_CH)

    def chunk_loop(fn, init):
        return lax.fori_loop(0, npair, lambda cp, carry: fn(2 * cp + 1, fn(2 * cp, carry)), init)
    w = misc_ref[:, MISC_WI:MISC_WI + N_IDX_HEADS] * (N_IDX_HEADS ** -0.5 * IDX_DIM ** -0.5)
    for jh in range(N_IDX_HEADS):
        wb_scr[jh] = jnp.broadcast_to(w[:, jh:jh + 1], (IDX_ROWS, LANES))
    row = t0 + lax.broadcasted_iota(jnp.int32, (IDX_ROWS, LANES), 0)
    lane = lax.broadcasted_iota(jnp.int32, (IDX_ROWS, LANES), 1)

    def score_chunk(c, carry):
        c0 = pl.multiple_of(c * IDX_CH, IDX_CH)
        kc = kit_ref[:, pl.ds(c0, IDX_CH)]
        z = jnp.zeros_like(kc)
        rhs = jnp.concatenate([jnp.concatenate([kc, z], axis=1), jnp.concatenate([z, kc], axis=1)], axis=0)
        accs = [jnp.zeros((IDX_ROWS, LANES), F32) for _ in range(nsub)]
        for p in range(N_IDX_HEADS // 2):
            s2 = jnp.dot(qi_ref[:, p * 2 * IDX_DIM:(p + 1) * 2 * IDX_DIM], rhs, preferred_element_type=F32)
            wa, wb = wb_scr[2 * p], wb_scr[2 * p + 1]
            for u in range(nsub):
                accs[u] = (accs[u] + wa * jnp.maximum(s2[:, u * LANES:(u + 1) * LANES], 0.0)
                           + wb * jnp.maximum(s2[:, IDX_CH + u * LANES:IDX_CH + (u + 1) * LANES], 0.0))
        for u in range(nsub):
            bits = pltpu.bitcast(accs[u], jnp.int32)
            key = bits ^ ((bits >> 31) & 0x7FFFFFFF)
            col = c0 + u * LANES + lane
            keys_scr[:, pl.ds(pl.multiple_of(c0 + u * LANES, LANES), LANES)] = jnp.where(col <= row, key, INT_MIN)
        return carry

    chunk_loop(score_chunk, 0)

    def count_ge(cand):
        def body(c, acc):
            c0 = pl.multiple_of(c * IDX_CH, IDX_CH)
            blk = keys_scr[:, pl.ds(c0, IDX_CH)]
            for u in range(nsub):
                acc = acc + jnp.where(blk[:, u * LANES:(u + 1) * LANES] >= cand, 1.0, 0.0)
            return acc
        acc = chunk_loop(body, jnp.zeros((IDX_ROWS, LANES), F32))
        return jnp.broadcast_to(jnp.sum(acc, axis=1, keepdims=True), (IDX_ROWS, LANES))

    def lane_all(x, op):
        s = LANES // 2
        while s >= 1:
            x = op(x, pltpu.roll(x, s, axis=1))
            s //= 2
        return x

    def gmax_body(c, gs):
        c0 = pl.multiple_of(c * IDX_CH, IDX_CH)
        blk = keys_scr[:, pl.ds(c0, IDX_CH)]
        return tuple(jnp.maximum(g, blk[:, u * LANES:(u + 1) * LANES]) for u, g in enumerate(gs))

    gs = chunk_loop(gmax_body, tuple(jnp.full((IDX_ROWS, LANES), INT_MIN, jnp.int32) for _ in range(nsub)))
    gmin, gmax = gs[0], gs[0]
    for g in gs[1:]:
        gmin, gmax = jnp.minimum(gmin, g), jnp.maximum(gmax, g)
    short = row < topk

    def all_true(flag):
        return (jnp.min(jnp.where(flag, 1.0, 0.0)) > 0.5).astype(jnp.int32)

    def settled(lo, hi, clo):
        return all_true(short | (clo == topk) | (hi - lo == 1))

    def bisect(state):
        it, lo, hi, clo, chi, _ = state
        mid = (lo >> 1) + (hi >> 1) + (lo & hi & 1)
        cnt = count_ge(mid)
        ge = cnt >= topk
        lo, hi = jnp.where(ge, mid, lo), jnp.where(ge, hi, mid)
        clo, chi = jnp.where(ge, cnt, clo), jnp.where(ge, chi, cnt)
        return it + 1, lo, hi, clo, chi, settled(lo, hi, clo)

    lo0 = lane_all(gmin, jnp.minimum)
    hi0 = lane_all(gmax, jnp.maximum) + 1
    clo0 = jnp.full((IDX_ROWS, LANES), -1.0, F32)
    chi0 = jnp.zeros((IDX_ROWS, LANES), F32)
    _, lo, _, clo, chi, _ = lax.while_loop(lambda st: (st[5] == 0) & (st[0] < 34), bisect,
                                           (jnp.int32(0), lo0, hi0, clo0, chi0, settled(lo0, hi0, clo0)))
    thr = jnp.where(short, INT_MIN + 1, jnp.maximum(lo, INT_MIN + 1))
    tied = jnp.logical_not(short) & (clo != topk)
    no_ties = all_true(jnp.logical_not(tied))

    def emit_chunks(select):
        def emit(c, carry):
            c0 = pl.multiple_of(c * IDX_CH, IDX_CH)
            blk = keys_scr[:, pl.ds(c0, IDX_CH)]
            for u in range(nsub):
                sel = select(blk[:, u * LANES:(u + 1) * LANES], c0 + u * LANES + lane)
                o_ref[:, pl.ds(pl.multiple_of(c0 + u * LANES, LANES), LANES)] = jnp.where(sel, 0.0, NEG_INF).astype(o_ref.dtype)
            return carry
        chunk_loop(emit, 0)

    @pl.when(no_ties == 1)
    def _():
        emit_chunks(lambda key, col: key >= thr)

    @pl.when(no_ties == 0)
    def _():
        need = topk - chi

        def count_tied_upto(col_max):
            def body(c, acc):
                c0 = pl.multiple_of(c * IDX_CH, IDX_CH)
                blk = keys_scr[:, pl.ds(c0, IDX_CH)]
                for u in range(nsub):
                    hit = (blk[:, u * LANES:(u + 1) * LANES] == lo) & (c0 + u * LANES + lane <= col_max)
                    acc = acc + jnp.where(hit, 1.0, 0.0)
                return acc
            acc = chunk_loop(body, jnp.zeros((IDX_ROWS, LANES), F32))
            return jnp.broadcast_to(jnp.sum(acc, axis=1, keepdims=True), (IDX_ROWS, LANES))

        def col_step(_, state):
            below, last = state
            mid = (below + last) >> 1
            enough = count_tied_upto(mid) >= need
            return jnp.where(enough, below, mid), jnp.where(enough, mid, last)

        steps = max(seq - 1, 1).bit_length() + 1
        _, last = lax.fori_loop(0, steps, col_step, (jnp.full((IDX_ROWS, LANES), -1, jnp.int32),
                                                      jnp.full((IDX_ROWS, LANES), seq - 1, jnp.int32)))
        untied = jnp.logical_not(tied)
        emit_chunks(lambda key, col: (tied & ((key > lo) | ((key == lo) & (col <= last)))) | (untied & (key >= thr)))

    def fill(c, carry):
        c0 = pl.multiple_of(c * IDX_CH, IDX_CH)
        o_ref[:, pl.ds(c0, IDX_CH)] = jnp.full((IDX_ROWS, IDX_CH), NEG_INF, o_ref.dtype)
        return carry

    lax.fori_loop(2 * npair, seq // IDX_CH, fill, 0)


def _index(qi, misc3, kit, topk):
    b, seq, w = qi.shape
    return pl.pallas_call(
        functools.partial(_index_kernel, topk=topk),
        out_shape=jax.ShapeDtypeStruct((b, seq, seq), BF16),
        grid=(b, seq // IDX_ROWS),
        in_specs=[pl.BlockSpec((None, IDX_ROWS, w), lambda bb, i: (bb, i, 0)),
                  pl.BlockSpec((None, IDX_ROWS, LANES), lambda bb, i: (bb, i, 0)),
                  pl.BlockSpec((None, IDX_DIM, seq), lambda bb, i: (bb, 0, 0))],
        out_specs=pl.BlockSpec((None, IDX_ROWS, seq), lambda bb, i: (bb, i, 0)),
        scratch_shapes=[pltpu.VMEM((IDX_ROWS, seq), jnp.int32),
                        pltpu.VMEM((N_IDX_HEADS, IDX_ROWS, LANES), F32)],
        compiler_params=_cparams(("parallel", "arbitrary")),
        name="index",
    )(qi, misc3, kit)


def _dsa_kernel(qi_ref, kj_ref, tab_ref, q_ref, k_ref, v_ref, b_ref, o_ref, m_scr, l_scr, acc_ref, toe_scr, s_scr):
    t = pl.program_id(1)
    i, j = qi_ref[t], kj_ref[t]
    tq, tk = q_ref.shape[0], k_ref.shape[0]
    r = tq // tk
    nbq, nbk = tq // LANES, tk // LANES

    @pl.when(j == 0)
    def _():
        _attn_init(m_scr, l_scr, acc_ref)

    @pl.when(t == 0)
    def _():
        row = lax.broadcasted_iota(jnp.int32, (LANES, LANES), 0)
        col = lax.broadcasted_iota(jnp.int32, (LANES, LANES), 1)
        d_diag, d_sub = row - col, LANES + row - col
        for h in range(N_DSA_HEADS):
            def body(d, carry, h=h):
                td, ts = carry
                val = tab_ref[h, d]
                return jnp.where(d_diag == d, val, td), jnp.where(d_sub == d, val, ts)
            zero = jnp.zeros((LANES, LANES), F32)
            td, ts = lax.fori_loop(0, LANES, body, (zero, zero))
            toe_scr[h, 0] = td
            toe_scr[h, 1] = ts

    def tile(e):
        near = [] if e is None else [(a, c, a - c - e * nbk) for a in range(nbq) for c in range(nbk)
                                     if a - c - e * nbk in (0, 1)]
        bias = b_ref[...].astype(F32)
        for h in range(N_DSA_HEADS):
            s = _qk(q_ref, k_ref, h) + bias
            if near:
                s_scr[...] = s
                for a, c, which in near:
                    s_scr[a * LANES:(a + 1) * LANES, c * LANES:(c + 1) * LANES] += toe_scr[h, which]
                s = s_scr[...]
            _softmax_update(s, v_ref[:, h * HEAD_DIM:(h + 1) * HEAD_DIM], m_scr, l_scr, acc_ref, h)

    @pl.when(j < r * i - 1)
    def _():
        tile(None)

    for e in range(-1, r):
        @pl.when(j == r * i + e)
        def _(e=e):
            tile(e)

    @pl.when(j == r * i + r - 1)
    def _():
        _attn_finish(o_ref, l_scr, acc_ref, N_DSA_HEADS)


def _dsa(tab, q, k, v, bias):
    b, seq, w = q.shape
    tq, tk = min(ATT_TQ, seq), min(DSA_TK, seq)
    qi, kj = _causal_pairs(seq, tq, tk)
    q_map = lambda bb, t, qi, kj: (bb, qi[t], 0)
    kv_map = lambda bb, t, qi, kj: (bb, kj[t], 0)
    return pl.pallas_call(
        _dsa_kernel,
        out_shape=jax.ShapeDtypeStruct((b, seq, w), BF16),
        grid_spec=pltpu.PrefetchScalarGridSpec(
            num_scalar_prefetch=2,
            grid=(b, qi.shape[0]),
            in_specs=[pl.BlockSpec(memory_space=pltpu.SMEM),
                      pl.BlockSpec((None, tq, w), q_map),
                      pl.BlockSpec((None, tk, w), kv_map),
                      pl.BlockSpec((None, tk, w), kv_map),
                      pl.BlockSpec((None, tq, tk), lambda bb, t, qi, kj: (bb, qi[t], kj[t]))],
            out_specs=pl.BlockSpec((None, tq, w), q_map),
            scratch_shapes=[pltpu.VMEM((N_DSA_HEADS, tq, LANES), F32),
                            pltpu.VMEM((N_DSA_HEADS, tq, LANES), F32),
                            pltpu.VMEM((tq, w), F32),
                            pltpu.VMEM((N_DSA_HEADS, 2, LANES, LANES), F32),
                            pltpu.VMEM((tq, tk), F32)]),
        compiler_params=_cparams(("arbitrary", "arbitrary")),
        name="dsa",
    )(qi, kj, tab, q, k, v, bias)


def _merge_kernel(af_ref, ad_ref, ga_ref, gb_ref, x_ref, mod_ref, wof_ref, wod_ref, wo_ref, g2_ref,
                  x1_ref, h2_ref):
    yf = jnp.dot(af_ref[...], wof_ref[...], preferred_element_type=F32)
    yd = jnp.dot(ad_ref[...], wod_ref[...], preferred_element_type=F32)
    merged = ga_ref[...].astype(F32) * yf + gb_ref[...].astype(F32) * yd
    o = jnp.dot(merged.astype(BF16), wo_ref[...], preferred_element_type=F32)
    x1 = x_ref[...] + mod_ref[2:3, :] * o
    x1_ref[...] = x1
    h2_ref[...] = _modulated_norm(x1, g2_ref[...], mod_ref[4:5, :], mod_ref[3:4, :]).astype(h2_ref.dtype)


def _merge(af, ad, ga, gb, x2, mod3, wof, wod, wo, g2, seq, tm=256):
    m, d = x2.shape
    per_b = seq // tm
    row = lambda i: (i, 0)
    const = lambda i: (0, 0)
    return pl.pallas_call(
        _merge_kernel,
        out_shape=(jax.ShapeDtypeStruct((m, d), F32), jax.ShapeDtypeStruct((m, d), BF16)),
        grid=(m // tm,),
        in_specs=[pl.BlockSpec((tm, FOX_W), row), pl.BlockSpec((tm, DSA_W), row),
                  pl.BlockSpec((tm, d), row), pl.BlockSpec((tm, d), lambda i: (i, 1)), pl.BlockSpec((tm, d), row),
                  pl.BlockSpec((None, 6, d), lambda i: (i // per_b, 0, 0)),
                  pl.BlockSpec((FOX_W, d), const), pl.BlockSpec((DSA_W, d), const), pl.BlockSpec((d, d), const),
                  pl.BlockSpec((1, d), const)],
        out_specs=(pl.BlockSpec((tm, d), row), pl.BlockSpec((tm, d), row)),
        compiler_params=_cparams(("parallel",)),
        name="merge",
    )(af, ad, ga, gb, x2, mod3, wof, wod, wo, g2)


FFN_HALO = 16


def _ffn_kernel(h_ref, halo_ref, wa_ref, wb_ref, cwa_ref, cwb_ref, cba_ref, cbb_ref, wout_ref, x1_ref, mod_ref,
                o_ref, hext_scr, acc_ref, *, per_b):
    i = pl.program_id(0)
    f = pl.program_id(1)
    tm = h_ref.shape[0]

    @pl.when(f == 0)
    def _():
        first = (i % per_b) == 0
        hext_scr[0:FFN_HALO, :] = jnp.where(first, jnp.zeros_like(halo_ref[...]), halo_ref[...])
        hext_scr[FFN_HALO:, :] = h_ref[...]
        acc_ref[...] = jnp.zeros_like(acc_ref)

    hext = hext_scr[...]

    def conv(w_ref, cw_ref, cb_ref):
        u = jnp.dot(hext, w_ref[...], preferred_element_type=F32)
        y = cw_ref[2:3, :] * u + cw_ref[1:2, :] * pltpu.roll(u, 1, axis=0) + cw_ref[0:1, :] * pltpu.roll(u, 2, axis=0)
        return y[FFN_HALO:, :] + cb_ref[...]

    ya = conv(wa_ref, cwa_ref, cba_ref)
    yb = conv(wb_ref, cwb_ref, cbb_ref)
    act = (ya * jax.nn.sigmoid(ya) * yb).astype(BF16)
    acc_ref[...] += jnp.dot(act, wout_ref[...], preferred_element_type=F32)

    @pl.when(f == pl.num_programs(1) - 1)
    def _():
        o_ref[...] = x1_ref[...] + mod_ref[5:6, :] * acc_ref[...]


def _ffn(h2, w_in, conv_w, conv_b, w_out, x1, mod3, seq, tm=512, tf=512):
    m, d = h2.shape
    dff = w_out.shape[0]
    nf = dff // tf
    per_b = seq // tm
    hb = tm // FFN_HALO
    return pl.pallas_call(
        functools.partial(_ffn_kernel, per_b=per_b),
        out_shape=jax.ShapeDtypeStruct((m, d), F32),
        grid=(m // tm, nf),
        in_specs=[pl.BlockSpec((tm, d), lambda i, f: (i, 0)),
                  pl.BlockSpec((FFN_HALO, d), lambda i, f: (jnp.maximum(i * hb - 1, 0), 0)),
                  pl.BlockSpec((d, tf), lambda i, f: (0, f)),
                  pl.BlockSpec((d, tf), lambda i, f: (0, f + nf)),
                  pl.BlockSpec((CONV_WIDTH, tf), lambda i, f: (0, f)),
                  pl.BlockSpec((CONV_WIDTH, tf), lambda i, f: (0, f + nf)),
                  pl.BlockSpec((1, tf), lambda i, f: (0, f)),
                  pl.BlockSpec((1, tf), lambda i, f: (0, f + nf)),
                  pl.BlockSpec((tf, d), lambda i, f: (f, 0)),
                  pl.BlockSpec((tm, d), lambda i, f: (i, 0)),
                  pl.BlockSpec((None, 6, d), lambda i, f: (i // per_b, 0, 0))],
        out_specs=pl.BlockSpec((tm, d), lambda i, f: (i, 0)),
        scratch_shapes=[pltpu.VMEM((tm + FFN_HALO, d), BF16), pltpu.VMEM((tm, d), F32)],
        compiler_params=_cparams(("parallel", "arbitrary")),
        name="ffn",
    )(h2, h2, w_in, w_in, conv_w, conv_w, conv_b, conv_b, w_out, x1, mod3)


def _t5_bucket(n):
    n = jnp.maximum(n, 0)
    max_exact = N_BUCKETS // 2
    nf = jnp.maximum(n, 1).astype(F32)
    large = max_exact + (jnp.log(nf / max_exact) / math.log(MAX_DISTANCE / max_exact)
                         * (N_BUCKETS - max_exact)).astype(jnp.int32)
    large = jnp.minimum(large, N_BUCKETS - 1)
    return jnp.where(n < max_exact, n, large)


def _layer(x, c8, w_ada, b_ada, norm1_g, w_in, b_forget, q_norm_fox, k_norm_fox, kv_norm_g, w_ukv, q_norm_dsa,
           k_norm_dsa, w_out_fox, w_out_dsa, w_out, norm2_g, w_ffn_in, conv_w, conv_b, w_ffn_out, rel_bias):
    b, seq, d = x.shape
    m = b * seq
    topk = min(TOPK_MAX, seq // 4)
    x2 = x.reshape(m, d)

    mod3 = _ada(c8, w_ada, b_ada.reshape(1, -1))[:b].reshape(b, 6, d)
    h1 = _normmod(x2, norm1_g.reshape(1, d), mod3, seq)

    o = 0
    cols = {}
    for name, size in (("qf", FOX_W), ("kf", FOX_W), ("vf", FOX_W), ("fg", N_FOX_HEADS), ("qd", DSA_W),
                       ("ckv", KV_LORA), ("qi", N_IDX_HEADS * IDX_DIM), ("ki", IDX_DIM), ("wi", N_IDX_HEADS),
                       ("ga", d), ("gb", d)):
        cols[name] = w_in[:, o:o + size]
        o += size
    wb = lambda a: a.astype(BF16)
    scale = HEAD_DIM ** -0.5 * LOG2E
    tile_h = lambda g, nh: jnp.tile(g.reshape(1, HEAD_DIM), (1, nh))

    qf = _proj(h1, wb(cols["qf"]), tile_h(q_norm_fox, N_FOX_HEADS) * scale, mode="headnorm", out_dtype=BF16, name="proj_qf")
    kf = _proj(h1, wb(cols["kf"]), tile_h(k_norm_fox, N_FOX_HEADS), mode="headnorm", out_dtype=BF16, name="proj_kf")
    vf = _proj(h1, wb(cols["vf"]), None, mode="plain", out_dtype=BF16, name="proj_vf")
    qd = _proj(h1, wb(cols["qd"]), tile_h(q_norm_dsa, N_DSA_HEADS) * scale, mode="headnorm", out_dtype=BF16, name="proj_qd")
    qi = _proj(h1, wb(cols["qi"]), None, mode="plain", out_dtype=BF16, name="proj_qi")
    gates = _proj(h1, wb(jnp.concatenate([cols["ga"], cols["gb"]], axis=1)), None, mode="sigmoid", out_dtype=BF16,
                  name="proj_gates")
    pad = jnp.zeros((d, LANES - N_FOX_HEADS - N_IDX_HEADS - IDX_DIM), w_in.dtype)
    misc = _proj(h1, wb(jnp.concatenate([cols["fg"], cols["wi"], cols["ki"], pad], axis=1)), None, mode="plain",
                 out_dtype=F32, name="proj_misc")
    kd, vd = _ckv(h1, wb(cols["ckv"]), kv_norm_g.reshape(1, KV_LORA), wb(w_ukv), tile_h(k_norm_dsa, N_DSA_HEADS))

    misc3 = misc.reshape(b, seq, LANES)
    bf = jnp.zeros((1, LANES), F32).at[0, :N_FOX_HEADS].set(b_forget.astype(F32))
    cum = _cumsum(misc3, bf)
    r3 = lambda a: a.reshape(b, seq, -1)
    qa, ka = _augment(qf, kf, cum.reshape(m, LANES))
    a_fox = _fox(r3(qa), r3(ka), r3(vf))

    kit = jnp.swapaxes(misc3[:, :, MISC_KI:MISC_KI + IDX_DIM], 1, 2).astype(BF16)
    sel_bias = _index(r3(qi), misc3, kit, topk)
    by_dist = rel_bias[_t5_bucket(jnp.arange(LANES, dtype=jnp.int32))] - rel_bias[N_BUCKETS - 1][None, :]
    a_dsa = _dsa((by_dist.T * LOG2E).astype(F32), r3(qd), r3(kd), r3(vd), sel_bias)

    x1, h2 = _merge(a_fox.reshape(m, FOX_W), a_dsa.reshape(m, DSA_W), gates, gates, x2, mod3,
                    wb(w_out_fox), wb(w_out_dsa), wb(w_out), norm2_g.reshape(1, d), seq)
    out = _ffn(h2, wb(w_ffn_in), conv_w, conv_b.reshape(1, -1), wb(w_ffn_out), x1, mod3, seq)
    return out.reshape(b, seq, d)


def kernel(x, c, w_ada, b_ada, norm1_g, w_in, b_forget, q_norm_fox, k_norm_fox, kv_norm_g, w_ukv, q_norm_dsa, k_norm_dsa, w_out_fox, w_out_dsa, w_out, norm2_g, w_ffn_in, conv_w, conv_b, w_ffn_out, rel_bias):
    b = x.shape[0]
    c8 = jnp.zeros((8, c.shape[1]), c.dtype).at[:b].set(c)
    for l in range(w_ada.shape[0]):
        x = _layer(x, c8, w_ada[l], b_ada[l], norm1_g[l], w_in[l], b_forget[l], q_norm_fox[l], k_norm_fox[l],
                   kv_norm_g[l], w_ukv[l], q_norm_dsa[l], k_norm_dsa[l], w_out_fox[l], w_out_dsa[l], w_out[l],
                   norm2_g[l], w_ffn_in[l], conv_w[l], conv_b[l], w_ffn_out[l], rel_bias)
    return x
```

```python
import functools
import math

import jax
import jax.numpy as jnp
from jax import lax
from jax.experimental import pallas as pl
from jax.experimental.pallas import tpu as pltpu

HEAD_DIM = 128
N_FOX_HEADS = 8
N_DSA_HEADS = 8
FOX_W = N_FOX_HEADS * HEAD_DIM
DSA_W = N_DSA_HEADS * HEAD_DIM
KV_LORA = 256
N_IDX_HEADS = 16
IDX_DIM = 64
TOPK_MAX = 256
N_BUCKETS = 32
MAX_DISTANCE = 128
CONV_WIDTH = 3
EPS = 1e-6
NEG_INF = -1e30
LOG2E = 1.4426950408889634

LANES = 128
INT_MIN = -(2 ** 31)
VMEM_LIMIT = 56 * 1024 * 1024

F32 = jnp.float32
BF16 = jnp.bfloat16

MISC_FG = 0
MISC_WI = N_FOX_HEADS
MISC_KI = MISC_WI + N_IDX_HEADS


def _cparams(sem):
    return pltpu.CompilerParams(dimension_semantics=sem, vmem_limit_bytes=VMEM_LIMIT)


def _ada_kernel(c_ref, w_ref, b_ref, o_ref):
    c = c_ref[...]
    ca = c * jax.nn.sigmoid(c)
    o_ref[...] = jnp.dot(ca, w_ref[...], preferred_element_type=F32,
                         precision=lax.Precision.HIGHEST) + b_ref[...]


def _ada(c8, w, b, tn=1024):
    rows, d = c8.shape
    n = w.shape[1]
    return pl.pallas_call(
        _ada_kernel,
        out_shape=jax.ShapeDtypeStruct((rows, n), F32),
        grid=(n // tn,),
        in_specs=[pl.BlockSpec((rows, d), lambda j: (0, 0)),
                  pl.BlockSpec((d, tn), lambda j: (0, j)),
                  pl.BlockSpec((1, tn), lambda j: (0, j))],
        out_specs=pl.BlockSpec((rows, tn), lambda j: (0, j)),
        compiler_params=_cparams(("arbitrary",)),
        name="ada",
    )(c8, w, b)


def _modulated_norm(x, g, sc, sh):
    ms = jnp.mean(x * x, axis=-1, keepdims=True)
    return (x * lax.rsqrt(ms + EPS) * g) * (1.0 + sc) + sh


def _normmod_kernel(x_ref, g_ref, mod_ref, o_ref):
    o_ref[...] = _modulated_norm(x_ref[...], g_ref[...], mod_ref[1:2, :], mod_ref[0:1, :]).astype(o_ref.dtype)


def _normmod(x2, g, mod3, seq, tm=1024):
    m, d = x2.shape
    per_b = seq // tm
    return pl.pallas_call(
        _normmod_kernel,
        out_shape=jax.ShapeDtypeStruct((m, d), BF16),
        grid=(m // tm,),
        in_specs=[pl.BlockSpec((tm, d), lambda i: (i, 0)),
                  pl.BlockSpec((1, d), lambda i: (0, 0)),
                  pl.BlockSpec((None, 6, d), lambda i: (i // per_b, 0, 0))],
        out_specs=pl.BlockSpec((tm, d), lambda i: (i, 0)),
        compiler_params=_cparams(("parallel",)),
        name="normmod",
    )(x2, g, mod3)


def _head_norm_store(acc, gain_ref, o_ref, col0=0):
    for hh in range(acc.shape[1] // HEAD_DIM):
        a = acc[:, hh * HEAD_DIM:(hh + 1) * HEAD_DIM]
        ms = jnp.mean(a * a, axis=-1, keepdims=True)
        sl = slice(col0 + hh * HEAD_DIM, col0 + (hh + 1) * HEAD_DIM)
        o_ref[:, sl] = (a * lax.rsqrt(ms + EPS) * gain_ref[:, hh * HEAD_DIM:(hh + 1) * HEAD_DIM]).astype(o_ref.dtype)


def _proj_kernel(h_ref, w_ref, gain_ref, o_ref, *, mode):
    acc = jnp.dot(h_ref[...], w_ref[...], preferred_element_type=F32)
    if mode == "headnorm":
        _head_norm_store(acc, gain_ref, o_ref)
    elif mode == "sigmoid":
        o_ref[...] = jax.nn.sigmoid(acc).astype(o_ref.dtype)
    else:
        o_ref[...] = acc.astype(o_ref.dtype)


def _proj(h, w, gain, *, mode, out_dtype, tm=1024, tn=1024, name="proj"):
    m, d = h.shape
    n = w.shape[1]
    tn = min(tn, n)
    if gain is None:
        gain = jnp.ones((1, n), F32)
    return pl.pallas_call(
        functools.partial(_proj_kernel, mode=mode),
        out_shape=jax.ShapeDtypeStruct((m, n), out_dtype),
        grid=(n // tn, m // tm),
        in_specs=[pl.BlockSpec((tm, d), lambda j, i: (i, 0)),
                  pl.BlockSpec((d, tn), lambda j, i: (0, j)),
                  pl.BlockSpec((1, tn), lambda j, i: (0, j))],
        out_specs=pl.BlockSpec((tm, tn), lambda j, i: (i, j)),
        compiler_params=_cparams(("parallel", "parallel")),
        name=name,
    )(h, w, gain)


def _ckv_kernel(h_ref, wc_ref, g_ref, wu_ref, gk_ref, k_ref, v_ref):
    c = jnp.dot(h_ref[...], wc_ref[...], preferred_element_type=F32)
    ms = jnp.mean(c * c, axis=-1, keepdims=True)
    cn = (c * lax.rsqrt(ms + EPS) * g_ref[...]).astype(BF16)
    kv = jnp.dot(cn, wu_ref[...], preferred_element_type=F32)
    _head_norm_store(kv[:, :DSA_W], gk_ref, k_ref)
    v_ref[...] = kv[:, DSA_W:].astype(v_ref.dtype)


def _ckv(h, wc, g, wu, gk, tm=1024):
    m, d = h.shape
    return pl.pallas_call(
        _ckv_kernel,
        out_shape=(jax.ShapeDtypeStruct((m, DSA_W), BF16), jax.ShapeDtypeStruct((m, DSA_W), BF16)),
        grid=(m // tm,),
        in_specs=[pl.BlockSpec((tm, d), lambda i: (i, 0)),
                  pl.BlockSpec((d, KV_LORA), lambda i: (0, 0)),
                  pl.BlockSpec((1, KV_LORA), lambda i: (0, 0)),
                  pl.BlockSpec((KV_LORA, 2 * DSA_W), lambda i: (0, 0)),
                  pl.BlockSpec((1, DSA_W), lambda i: (0, 0))],
        out_specs=(pl.BlockSpec((tm, DSA_W), lambda i: (i, 0)), pl.BlockSpec((tm, DSA_W), lambda i: (i, 0))),
        compiler_params=_cparams(("parallel",)),
        name="ckv",
    )(h, wc, g, wu, gk)


def _cum_kernel(m_ref, bf_ref, o_ref, carry_ref):
    @pl.when(pl.program_id(1) == 0)
    def _():
        carry_ref[...] = jnp.zeros_like(carry_ref)

    z = m_ref[...] + bf_ref[...]
    lf = -(jnp.maximum(-z, 0.0) + jnp.log1p(jnp.exp(-jnp.abs(z))))
    tc = lf.shape[0]
    row = lax.broadcasted_iota(jnp.int32, lf.shape, 0)
    s = 1
    while s < tc:
        lf = lf + jnp.where(row >= s, pltpu.roll(lf, s, axis=0), 0.0)
        s *= 2
    out = lf + carry_ref[0:1, :]
    o_ref[...] = out
    carry_ref[...] = jnp.broadcast_to(out[tc - 1:tc, :], carry_ref.shape)


def _cumsum(misc3, bf, tc=512):
    b, seq, w = misc3.shape
    return pl.pallas_call(
        _cum_kernel,
        out_shape=jax.ShapeDtypeStruct((b, seq, w), F32),
        grid=(b, seq // tc),
        in_specs=[pl.BlockSpec((None, tc, w), lambda bb, i: (bb, i, 0)),
                  pl.BlockSpec((1, w), lambda bb, i: (0, 0))],
        out_specs=pl.BlockSpec((None, tc, w), lambda bb, i: (bb, i, 0)),
        scratch_shapes=[pltpu.VMEM((8, w), F32)],
        compiler_params=_cparams(("arbitrary", "arbitrary")),
        name="cumsum",
    )(misc3, bf)


def _softmax_update(s, v_h, m_scr, l_scr, acc_ref, h):
    m_prev = m_scr[h]
    m_new = jnp.maximum(m_prev, jnp.max(s, axis=1, keepdims=True))
    alpha = jnp.exp2(m_prev - m_new)
    p = jnp.exp2(s - m_new[:, :1]).astype(BF16)
    v_ones = jnp.concatenate([v_h, jnp.ones_like(v_h)], axis=1)
    pv = jnp.dot(p, v_ones, preferred_element_type=F32)
    l_scr[h] = alpha * l_scr[h] + pv[:, HEAD_DIM:]
    m_scr[h] = m_new
    sl = slice(h * HEAD_DIM, (h + 1) * HEAD_DIM)
    acc_ref[:, sl] = acc_ref[:, sl] * alpha + pv[:, :HEAD_DIM]


def _attn_init(m_scr, l_scr, acc_ref):
    m_scr[...] = jnp.full(m_scr.shape, NEG_INF, F32)
    l_scr[...] = jnp.zeros(l_scr.shape, F32)
    acc_ref[...] = jnp.zeros(acc_ref.shape, F32)


def _attn_finish(o_ref, l_scr, acc_ref, nheads):
    for h in range(nheads):
        sl = slice(h * HEAD_DIM, (h + 1) * HEAD_DIM)
        o_ref[:, sl] = (acc_ref[:, sl] / l_scr[h]).astype(o_ref.dtype)


def _qk(q_ref, k_ref, h, width=HEAD_DIM):
    sl = slice(h * width, (h + 1) * width)
    return lax.dot_general(q_ref[:, sl], k_ref[:, sl], (((1,), (1,)), ((), ())), preferred_element_type=F32)


AUG_W = 2 * HEAD_DIM
N_PIECES = 3


def _aug_kernel(q_ref, k_ref, cum_ref, selq_ref, selk_ref, cq_ref, ck_ref, qa_ref, ka_ref):
    lane = lax.broadcasted_iota(jnp.int32, cum_ref.shape, 1)
    c = jnp.where(lane < N_FOX_HEADS, cum_ref[...] * LOG2E, 0.0)
    hi = c.astype(BF16).astype(F32)
    r1 = c - hi
    mid = r1.astype(BF16).astype(F32)
    lo = (r1 - mid).astype(BF16).astype(F32)
    pieces = (hi + pltpu.roll(mid, N_FOX_HEADS, axis=1) + pltpu.roll(lo, 2 * N_FOX_HEADS, axis=1)).astype(BF16)
    for h in range(N_FOX_HEADS):
        sl = slice(h * HEAD_DIM, (h + 1) * HEAD_DIM)
        qa_ref[:, h * AUG_W:h * AUG_W + HEAD_DIM] = q_ref[:, sl]
        ka_ref[:, h * AUG_W:h * AUG_W + HEAD_DIM] = k_ref[:, sl]
        eq = jnp.dot(pieces, selq_ref[h], preferred_element_type=F32) + cq_ref[...]
        ek = jnp.dot(pieces, selk_ref[h], preferred_element_type=F32) + ck_ref[...]
        qa_ref[:, h * AUG_W + HEAD_DIM:(h + 1) * AUG_W] = eq.astype(BF16)
        ka_ref[:, h * AUG_W + HEAD_DIM:(h + 1) * AUG_W] = ek.astype(BF16)


def _aug_tables():
    selq = [[[0.0] * LANES for _ in range(LANES)] for _ in range(N_FOX_HEADS)]
    selk = [[[0.0] * LANES for _ in range(LANES)] for _ in range(N_FOX_HEADS)]
    for h in range(N_FOX_HEADS):
        for p in range(N_PIECES):
            selq[h][p * N_FOX_HEADS + h][p] = 1.0
            selk[h][p * N_FOX_HEADS + h][N_PIECES + p] = -1.0
    cq = [[1.0 if N_PIECES <= c < 2 * N_PIECES else 0.0 for c in range(LANES)]]
    ck = [[1.0 if c < N_PIECES else 0.0 for c in range(LANES)]]
    return (jnp.array(selq, BF16), jnp.array(selk, BF16), jnp.array(cq, F32), jnp.array(ck, F32))


def _augment(q, k, cum2, tm=1024):
    m, w = q.shape
    selq, selk, cq, ck = _aug_tables()
    row = lambda i: (i, 0)
    const3 = lambda i: (0, 0, 0)
    const2 = lambda i: (0, 0)
    wa = N_FOX_HEADS * AUG_W
    return pl.pallas_call(
        _aug_kernel,
        out_shape=(jax.ShapeDtypeStruct((m, wa), BF16), jax.ShapeDtypeStruct((m, wa), BF16)),
        grid=(m // tm,),
        in_specs=[pl.BlockSpec((tm, w), row), pl.BlockSpec((tm, w), row), pl.BlockSpec((tm, LANES), row),
                  pl.BlockSpec((N_FOX_HEADS, LANES, LANES), const3), pl.BlockSpec((N_FOX_HEADS, LANES, LANES), const3),
                  pl.BlockSpec((1, LANES), const2), pl.BlockSpec((1, LANES), const2)],
        out_specs=(pl.BlockSpec((tm, wa), row), pl.BlockSpec((tm, wa), row)),
        compiler_params=_cparams(("parallel",)),
        name="augment",
    )(q, k, cum2, selq, selk, cq, ck)


ATT_TQ = 1024
FOX_TK = 1024
DSA_TK = 512


def _causal_pairs(seq, tq, tk):
    r = tq // tk
    pairs = [(i, j) for i in range(seq // tq) for j in range(r * (i + 1))]
    return (jnp.array([p[0] for p in pairs], jnp.int32), jnp.array([p[1] for p in pairs], jnp.int32))


def _fox_kernel(qi_ref, kj_ref, q_ref, k_ref, v_ref, o_ref, m_scr, l_scr, acc_ref):
    t = pl.program_id(1)
    i, j = qi_ref[t], kj_ref[t]
    tq, tk = q_ref.shape[0], k_ref.shape[0]
    r = tq // tk

    @pl.when(j == 0)
    def _():
        _attn_init(m_scr, l_scr, acc_ref)

    def tile(masked):
        if masked:
            col = lax.broadcasted_iota(jnp.int32, (tq, tk), 1) + (j * tk - i * tq)
            keep = col <= lax.broadcasted_iota(jnp.int32, (tq, tk), 0)
        for h in range(N_FOX_HEADS):
            s = _qk(q_ref, k_ref, h, AUG_W)
            if masked:
                s = jnp.where(keep, s, NEG_INF)
            _softmax_update(s, v_ref[:, h * HEAD_DIM:(h + 1) * HEAD_DIM], m_scr, l_scr, acc_ref, h)

    @pl.when(j < r * i)
    def _():
        tile(False)

    @pl.when(j >= r * i)
    def _():
        tile(True)

    @pl.when(j == r * i + r - 1)
    def _():
        _attn_finish(o_ref, l_scr, acc_ref, N_FOX_HEADS)


def _fox(qa, ka, v):
    b, seq, w = v.shape
    wa = qa.shape[2]
    tq, tk = min(ATT_TQ, seq), min(FOX_TK, seq)
    qi, kj = _causal_pairs(seq, tq, tk)
    q_map = lambda bb, t, qi, kj: (bb, qi[t], 0)
    kv_map = lambda bb, t, qi, kj: (bb, kj[t], 0)
    return pl.pallas_call(
        _fox_kernel,
        out_shape=jax.ShapeDtypeStruct((b, seq, w), BF16),
        grid_spec=pltpu.PrefetchScalarGridSpec(
            num_scalar_prefetch=2,
            grid=(b, qi.shape[0]),
            in_specs=[pl.BlockSpec((None, tq, wa), q_map),
                      pl.BlockSpec((None, tk, wa), kv_map),
                      pl.BlockSpec((None, tk, w), kv_map)],
            out_specs=pl.BlockSpec((None, tq, w), q_map),
            scratch_shapes=[pltpu.VMEM((N_FOX_HEADS, tq, LANES), F32),
                            pltpu.VMEM((N_FOX_HEADS, tq, LANES), F32),
                            pltpu.VMEM((tq, w), F32)]),
        compiler_params=_cparams(("parallel", "arbitrary")),
        name="fox",
    )(qi, kj, qa, ka, v)


IDX_ROWS = 128
IDX_CH = 256


def _index_kernel(qi_ref, misc_ref, kit_ref, o_ref, keys_scr, wb_scr, *, topk):
    i = pl.program_id(1)
    seq = kit_ref.shape[1]
    nsub = IDX_CH // LANES
    t0 = i * IDX_ROWS
    npair = (t0 + IDX_ROWS + 2 * IDX_CH - 1) // (2 * IDX_CH)

    def chunk_loop(fn, init):
        return lax.fori_loop(0, npair, lambda cp, carry: fn(2 * cp + 1, fn(2 * cp, carry)), init)
    w = misc_ref[:, MISC_WI:MISC_WI + N_IDX_HEADS] * (N_IDX_HEADS ** -0.5 * IDX_DIM ** -0.5)
    for jh in range(N_IDX_HEADS):
        wb_scr[jh] = jnp.broadcast_to(w[:, jh:jh + 1], (IDX_ROWS, LANES))
    row = t0 + lax.broadcasted_iota(jnp.int32, (IDX_ROWS, LANES), 0)
    lane = lax.broadcasted_iota(jnp.int32, (IDX_ROWS, LANES), 1)

    def score_chunk(c, carry):
        c0 = pl.multiple_of(c * IDX_CH, IDX_CH)
        kc = kit_ref[:, pl.ds(c0, IDX_CH)]
        z = jnp.zeros_like(kc)
        rhs = jnp.concatenate([jnp.concatenate([kc, z], axis=1), jnp.concatenate([z, kc], axis=1)], axis=0)
        accs = [jnp.zeros((IDX_ROWS, LANES), F32) for _ in range(nsub)]
        for p in range(N_IDX_HEADS // 2):
            s2 = jnp.dot(qi_ref[:, p * 2 * IDX_DIM:(p + 1) * 2 * IDX_DIM], rhs, preferred_element_type=F32)
            wa, wb = wb_scr[2 * p], wb_scr[2 * p + 1]
            for u in range(nsub):
                accs[u] = (accs[u] + wa * jnp.maximum(s2[:, u * LANES:(u + 1) * LANES], 0.0)
                           + wb * jnp.maximum(s2[:, IDX_CH + u * LANES:IDX_CH + (u + 1) * LANES], 0.0))
        for u in range(nsub):
            bits = pltpu.bitcast(accs[u], jnp.int32)
            key = bits ^ ((bits >> 31) & 0x7FFFFFFF)
            col = c0 + u * LANES + lane
            keys_scr[:, pl.ds(pl.multiple_of(c0 + u * LANES, LANES), LANES)] = jnp.where(col <= row, key, INT_MIN)
        return carry

    chunk_loop(score_chunk, 0)

    def count_ge(cand):
        def body(c, acc):
            c0 = pl.multiple_of(c * IDX_CH, IDX_CH)
            blk = keys_scr[:, pl.ds(c0, IDX_CH)]
            for u in range(nsub):
                acc = acc + jnp.where(blk[:, u * LANES:(u + 1) * LANES] >= cand, 1.0, 0.0)
            return acc
        acc = chunk_loop(body, jnp.zeros((IDX_ROWS, LANES), F32))
        return jnp.broadcast_to(jnp.sum(acc, axis=1, keepdims=True), (IDX_ROWS, LANES))

    def lane_all(x, op):
        s = LANES // 2
        while s >= 1:
            x = op(x, pltpu.roll(x, s, axis=1))
            s //= 2
        return x

    def gmax_body(c, gs):
        c0 = pl.multiple_of(c * IDX_CH, IDX_CH)
        blk = keys_scr[:, pl.ds(c0, IDX_CH)]
        return tuple(jnp.maximum(g, blk[:, u * LANES:(u + 1) * LANES]) for u, g in enumerate(gs))

    gs = chunk_loop(gmax_body, tuple(jnp.full((IDX_ROWS, LANES), INT_MIN, jnp.int32) for _ in range(nsub)))
    gmin, gmax = gs[0], gs[0]
    for g in gs[1:]:
        gmin, gmax = jnp.minimum(gmin, g), jnp.maximum(gmax, g)
    short = row < topk

    def all_true(flag):
        return (jnp.min(jnp.where(flag, 1.0, 0.0)) > 0.5).astype(jnp.int32)

    def settled(lo, hi, clo):
        return all_true(short | (clo == topk) | (hi - lo == 1))

    def bisect(state):
        it, lo, hi, clo, chi, _ = state
        mid = (lo >> 1) + (hi >> 1) + (lo & hi & 1)
        cnt = count_ge(mid)
        ge = cnt >= topk
        lo, hi = jnp.where(ge, mid, lo), jnp.where(ge, hi, mid)
        clo, chi = jnp.where(ge, cnt, clo), jnp.where(ge, chi, cnt)
        return it + 1, lo, hi, clo, chi, settled(lo, hi, clo)

    lo0 = lane_all(gmin, jnp.minimum)
    hi0 = lane_all(gmax, jnp.maximum) + 1
    clo0 = jnp.full((IDX_ROWS, LANES), -1.0, F32)
    chi0 = jnp.zeros((IDX_ROWS, LANES), F32)
    _, lo, _, clo, chi, _ = lax.while_loop(lambda st: (st[5] == 0) & (st[0] < 34), bisect,
                                           (jnp.int32(0), lo0, hi0, clo0, chi0, settled(lo0, hi0, clo0)))
    thr = jnp.where(short, INT_MIN + 1, jnp.maximum(lo, INT_MIN + 1))
    tied = jnp.logical_not(short) & (clo != topk)
    no_ties = all_true(jnp.logical_not(tied))

    def emit_chunks(select):
        def emit(c, carry):
            c0 = pl.multiple_of(c * IDX_CH, IDX_CH)
            blk = keys_scr[:, pl.ds(c0, IDX_CH)]
            for u in range(nsub):
                sel = select(blk[:, u * LANES:(u + 1) * LANES], c0 + u * LANES + lane)
                o_ref[:, pl.ds(pl.multiple_of(c0 + u * LANES, LANES), LANES)] = jnp.where(sel, 0.0, NEG_INF).astype(o_ref.dtype)
            return carry
        chunk_loop(emit, 0)

    @pl.when(no_ties == 1)
    def _():
        emit_chunks(lambda key, col: key >= thr)

    @pl.when(no_ties == 0)
    def _():
        need = topk - chi

        def count_tied_upto(col_max):
            def body(c, acc):
                c0 = pl.multiple_of(c * IDX_CH, IDX_CH)
                blk = keys_scr[:, pl.ds(c0, IDX_CH)]
                for u in range(nsub):
                    hit = (blk[:, u * LANES:(u + 1) * LANES] == lo) & (c0 + u * LANES + lane <= col_max)
                    acc = acc + jnp.where(hit, 1.0, 0.0)
                return acc
            acc = chunk_loop(body, jnp.zeros((IDX_ROWS, LANES), F32))
            return jnp.broadcast_to(jnp.sum(acc, axis=1, keepdims=True), (IDX_ROWS, LANES))

        def col_step(_, state):
            below, last = state
            mid = (below + last) >> 1
            enough = count_tied_upto(mid) >= need
            return jnp.where(enough, below, mid), jnp.where(enough, mid, last)

        steps = max(seq - 1, 1).bit_length() + 1
        _, last = lax.fori_loop(0, steps, col_step, (jnp.full((IDX_ROWS, LANES), -1, jnp.int32),
                                                      jnp.full((IDX_ROWS, LANES), seq - 1, jnp.int32)))
        untied = jnp.logical_not(tied)
        emit_chunks(lambda key, col: (tied & ((key > lo) | ((key == lo) & (col <= last)))) | (untied & (key >= thr)))

    def fill(c, carry):
        c0 = pl.multiple_of(c * IDX_CH, IDX_CH)
        o_ref[:, pl.ds(c0, IDX_CH)] = jnp.full((IDX_ROWS, IDX_CH), NEG_INF, o_ref.dtype)
        return carry

    lax.fori_loop(2 * npair, seq // IDX_CH, fill, 0)


def _index(qi, misc3, kit, topk):
    b, seq, w = qi.shape
    return pl.pallas_call(
        functools.partial(_index_kernel, topk=topk),
        out_shape=jax.ShapeDtypeStruct((b, seq, seq), BF16),
        grid=(b, seq // IDX_ROWS),
        in_specs=[pl.BlockSpec((None, IDX_ROWS, w), lambda bb, i: (bb, i, 0)),
                  pl.BlockSpec((None, IDX_ROWS, LANES), lambda bb, i: (bb, i, 0)),
                  pl.BlockSpec((None, IDX_DIM, seq), lambda bb, i: (bb, 0, 0))],
        out_specs=pl.BlockSpec((None, IDX_ROWS, seq), lambda bb, i: (bb, i, 0)),
        scratch_shapes=[pltpu.VMEM((IDX_ROWS, seq), jnp.int32),
                        pltpu.VMEM((N_IDX_HEADS, IDX_ROWS, LANES), F32)],
        compiler_params=_cparams(("parallel", "arbitrary")),
        name="index",
    )(qi, misc3, kit)


def _dsa_kernel(qi_ref, kj_ref, tab_ref, q_ref, k_ref, v_ref, b_ref, o_ref, m_scr, l_scr, acc_ref, toe_scr, s_scr):
    t = pl.program_id(1)
    i, j = qi_ref[t], kj_ref[t]
    tq, tk = q_ref.shape[0], k_ref.shape[0]
    r = tq // tk
    nbq, nbk = tq // LANES, tk // LANES

    @pl.when(j == 0)
    def _():
        _attn_init(m_scr, l_scr, acc_ref)

    @pl.when(t == 0)
    def _():
        row = lax.broadcasted_iota(jnp.int32, (LANES, LANES), 0)
        col = lax.broadcasted_iota(jnp.int32, (LANES, LANES), 1)
        d_diag, d_sub = row - col, LANES + row - col
        for h in range(N_DSA_HEADS):
            def body(d, carry, h=h):
                td, ts = carry
                val = tab_ref[h, d]
                return jnp.where(d_diag == d, val, td), jnp.where(d_sub == d, val, ts)
            zero = jnp.zeros((LANES, LANES), F32)
            td, ts = lax.fori_loop(0, LANES, body, (zero, zero))
            toe_scr[h, 0] = td
            toe_scr[h, 1] = ts

    def tile(e):
        near = [] if e is None else [(a, c, a - c - e * nbk) for a in range(nbq) for c in range(nbk)
                                     if a - c - e * nbk in (0, 1)]
        bias = b_ref[...].astype(F32)
        for h in range(N_DSA_HEADS):
            s = _qk(q_ref, k_ref, h) + bias
            if near:
                s_scr[...] = s
                for a, c, which in near:
                    s_scr[a * LANES:(a + 1) * LANES, c * LANES:(c + 1) * LANES] += toe_scr[h, which]
                s = s_scr[...]
            _softmax_update(s, v_ref[:, h * HEAD_DIM:(h + 1) * HEAD_DIM], m_scr, l_scr, acc_ref, h)

    @pl.when(j < r * i - 1)
    def _():
        tile(None)

    for e in range(-1, r):
        @pl.when(j == r * i + e)
        def _(e=e):
            tile(e)

    @pl.when(j == r * i + r - 1)
    def _():
        _attn_finish(o_ref, l_scr, acc_ref, N_DSA_HEADS)


def _dsa(tab, q, k, v, bias):
    b, seq, w = q.shape
    tq, tk = min(ATT_TQ, seq), min(DSA_TK, seq)
    qi, kj = _causal_pairs(seq, tq, tk)
    q_map = lambda bb, t, qi, kj: (bb, qi[t], 0)
    kv_map = lambda bb, t, qi, kj: (bb, kj[t], 0)
    return pl.pallas_call(
        _dsa_kernel,
        out_shape=jax.ShapeDtypeStruct((b, seq, w), BF16),
        grid_spec=pltpu.PrefetchScalarGridSpec(
            num_scalar_prefetch=2,
            grid=(b, qi.shape[0]),
            in_specs=[pl.BlockSpec(memory_space=pltpu.SMEM),
                      pl.BlockSpec((None, tq, w), q_map),
                      pl.BlockSpec((None, tk, w), kv_map),
                      pl.BlockSpec((None, tk, w), kv_map),
                      pl.BlockSpec((None, tq, tk), lambda bb, t, qi, kj: (bb, qi[t], kj[t]))],
            out_specs=pl.BlockSpec((None, tq, w), q_map),
            scratch_shapes=[pltpu.VMEM((N_DSA_HEADS, tq, LANES), F32),
                            pltpu.VMEM((N_DSA_HEADS, tq, LANES), F32),
                            pltpu.VMEM((tq, w), F32),
                            pltpu.VMEM((N_DSA_HEADS, 2, LANES, LANES), F32),
                            pltpu.VMEM((tq, tk), F32)]),
        compiler_params=_cparams(("arbitrary", "arbitrary")),
        name="dsa",
    )(qi, kj, tab, q, k, v, bias)


def _merge_kernel(af_ref, ad_ref, ga_ref, gb_ref, x_ref, mod_ref, wof_ref, wod_ref, wo_ref, g2_ref,
                  x1_ref, h2_ref):
    yf = jnp.dot(af_ref[...], wof_ref[...], preferred_element_type=F32)
    yd = jnp.dot(ad_ref[...], wod_ref[...], preferred_element_type=F32)
    merged = ga_ref[...].astype(F32) * yf + gb_ref[...].astype(F32) * yd
    o = jnp.dot(merged.astype(BF16), wo_ref[...], preferred_element_type=F32)
    x1 = x_ref[...] + mod_ref[2:3, :] * o
    x1_ref[...] = x1
    h2_ref[...] = _modulated_norm(x1, g2_ref[...], mod_ref[4:5, :], mod_ref[3:4, :]).astype(h2_ref.dtype)


def _merge(af, ad, ga, gb, x2, mod3, wof, wod, wo, g2, seq, tm=256):
    m, d = x2.shape
    per_b = seq // tm
    row = lambda i: (i, 0)
    const = lambda i: (0, 0)
    return pl.pallas_call(
        _merge_kernel,
        out_shape=(jax.ShapeDtypeStruct((m, d), F32), jax.ShapeDtypeStruct((m, d), BF16)),
        grid=(m // tm,),
        in_specs=[pl.BlockSpec((tm, FOX_W), row), pl.BlockSpec((tm, DSA_W), row),
                  pl.BlockSpec((tm, d), row), pl.BlockSpec((tm, d), lambda i: (i, 1)), pl.BlockSpec((tm, d), row),
                  pl.BlockSpec((None, 6, d), lambda i: (i // per_b, 0, 0)),
                  pl.BlockSpec((FOX_W, d), const), pl.BlockSpec((DSA_W, d), const), pl.BlockSpec((d, d), const),
                  pl.BlockSpec((1, d), const)],
        out_specs=(pl.BlockSpec((tm, d), row), pl.BlockSpec((tm, d), row)),
        compiler_params=_cparams(("parallel",)),
        name="merge",
    )(af, ad, ga, gb, x2, mod3, wof, wod, wo, g2)


FFN_HALO = 16


def _ffn_kernel(h_ref, halo_ref, wa_ref, wb_ref, cwa_ref, cwb_ref, cba_ref, cbb_ref, wout_ref, x1_ref, mod_ref,
                o_ref, hext_scr, acc_ref, *, per_b):
    i = pl.program_id(0)
    f = pl.program_id(1)
    tm = h_ref.shape[0]

    @pl.when(f == 0)
    def _():
        first = (i % per_b) == 0
        hext_scr[0:FFN_HALO, :] = jnp.where(first, jnp.zeros_like(halo_ref[...]), halo_ref[...])
        hext_scr[FFN_HALO:, :] = h_ref[...]
        acc_ref[...] = jnp.zeros_like(acc_ref)

    hext = hext_scr[...]

    def conv(w_ref, cw_ref, cb_ref):
        u = jnp.dot(hext, w_ref[...], preferred_element_type=F32)
        y = cw_ref[2:3, :] * u + cw_ref[1:2, :] * pltpu.roll(u, 1, axis=0) + cw_ref[0:1, :] * pltpu.roll(u, 2, axis=0)
        return y[FFN_HALO:, :] + cb_ref[...]

    ya = conv(wa_ref, cwa_ref, cba_ref)
    yb = conv(wb_ref, cwb_ref, cbb_ref)
    act = (ya * jax.nn.sigmoid(ya) * yb).astype(BF16)
    acc_ref[...] += jnp.dot(act, wout_ref[...], preferred_element_type=F32)

    @pl.when(f == pl.num_programs(1) - 1)
    def _():
        o_ref[...] = x1_ref[...] + mod_ref[5:6, :] * acc_ref[...]


def _ffn(h2, w_in, conv_w, conv_b, w_out, x1, mod3, seq, tm=512, tf=512):
    m, d = h2.shape
    dff = w_out.shape[0]
    nf = dff // tf
    per_b = seq // tm
    hb = tm // FFN_HALO
    return pl.pallas_call(
        functools.partial(_ffn_kernel, per_b=per_b),
        out_shape=jax.ShapeDtypeStruct((m, d), F32),
        grid=(m // tm, nf),
        in_specs=[pl.BlockSpec((tm, d), lambda i, f: (i, 0)),
                  pl.BlockSpec((FFN_HALO, d), lambda i, f: (jnp.maximum(i * hb - 1, 0), 0)),
                  pl.BlockSpec((d, tf), lambda i, f: (0, f)),
                  pl.BlockSpec((d, tf), lambda i, f: (0, f + nf)),
                  pl.BlockSpec((CONV_WIDTH, tf), lambda i, f: (0, f)),
                  pl.BlockSpec((CONV_WIDTH, tf), lambda i, f: (0, f + nf)),
                  pl.BlockSpec((1, tf), lambda i, f: (0, f)),
                  pl.BlockSpec((1, tf), lambda i, f: (0, f + nf)),
                  pl.BlockSpec((tf, d), lambda i, f: (f, 0)),
                  pl.BlockSpec((tm, d), lambda i, f: (i, 0)),
                  pl.BlockSpec((None, 6, d), lambda i, f: (i // per_b, 0, 0))],
        out_specs=pl.BlockSpec((tm, d), lambda i, f: (i, 0)),
        scratch_shapes=[pltpu.VMEM((tm + FFN_HALO, d), BF16), pltpu.VMEM((tm, d), F32)],
        compiler_params=_cparams(("parallel", "arbitrary")),
        name="ffn",
    )(h2, h2, w_in, w_in, conv_w, conv_w, conv_b, conv_b, w_out, x1, mod3)


def _t5_bucket(n):
    n = jnp.maximum(n, 0)
    max_exact = N_BUCKETS // 2
    nf = jnp.maximum(n, 1).astype(F32)
    large = max_exact + (jnp.log(nf / max_exact) / math.log(MAX_DISTANCE / max_exact)
                         * (N_BUCKETS - max_exact)).astype(jnp.int32)
    large = jnp.minimum(large, N_BUCKETS - 1)
    return jnp.where(n < max_exact, n, large)


def _layer(x, c8, w_ada, b_ada, norm1_g, w_in, b_forget, q_norm_fox, k_norm_fox, kv_norm_g, w_ukv, q_norm_dsa,
           k_norm_dsa, w_out_fox, w_out_dsa, w_out, norm2_g, w_ffn_in, conv_w, conv_b, w_ffn_out, rel_bias):
    b, seq, d = x.shape
    m = b * seq
    topk = min(TOPK_MAX, seq // 4)
    x2 = x.reshape(m, d)

    mod3 = _ada(c8, w_ada, b_ada.reshape(1, -1))[:b].reshape(b, 6, d)
    h1 = _normmod(x2, norm1_g.reshape(1, d), mod3, seq)

    o = 0
    cols = {}
    for name, size in (("qf", FOX_W), ("kf", FOX_W), ("vf", FOX_W), ("fg", N_FOX_HEADS), ("qd", DSA_W),
                       ("ckv", KV_LORA), ("qi", N_IDX_HEADS * IDX_DIM), ("ki", IDX_DIM), ("wi", N_IDX_HEADS),
                       ("ga", d), ("gb", d)):
        cols[name] = w_in[:, o:o + size]
        o += size
    wb = lambda a: a.astype(BF16)
    scale = HEAD_DIM ** -0.5 * LOG2E
    tile_h = lambda g, nh: jnp.tile(g.reshape(1, HEAD_DIM), (1, nh))

    qf = _proj(h1, wb(cols["qf"]), tile_h(q_norm_fox, N_FOX_HEADS) * scale, mode="headnorm", out_dtype=BF16, name="proj_qf")
    kf = _proj(h1, wb(cols["kf"]), tile_h(k_norm_fox, N_FOX_HEADS), mode="headnorm", out_dtype=BF16, name="proj_kf")
    vf = _proj(h1, wb(cols["vf"]), None, mode="plain", out_dtype=BF16, name="proj_vf")
    qd = _proj(h1, wb(cols["qd"]), tile_h(q_norm_dsa, N_DSA_HEADS) * scale, mode="headnorm", out_dtype=BF16, name="proj_qd")
    qi = _proj(h1, wb(cols["qi"]), None, mode="plain", out_dtype=BF16, name="proj_qi")
    gates = _proj(h1, wb(jnp.concatenate([cols["ga"], cols["gb"]], axis=1)), None, mode="sigmoid", out_dtype=BF16,
                  name="proj_gates")
    pad = jnp.zeros((d, LANES - N_FOX_HEADS - N_IDX_HEADS - IDX_DIM), w_in.dtype)
    misc = _proj(h1, wb(jnp.concatenate([cols["fg"], cols["wi"], cols["ki"], pad], axis=1)), None, mode="plain",
                 out_dtype=F32, name="proj_misc")
    kd, vd = _ckv(h1, wb(cols["ckv"]), kv_norm_g.reshape(1, KV_LORA), wb(w_ukv), tile_h(k_norm_dsa, N_DSA_HEADS))

    misc3 = misc.reshape(b, seq, LANES)
    bf = jnp.zeros((1, LANES), F32).at[0, :N_FOX_HEADS].set(b_forget.astype(F32))
    cum = _cumsum(misc3, bf)
    r3 = lambda a: a.reshape(b, seq, -1)
    qa, ka = _augment(qf, kf, cum.reshape(m, LANES))
    a_fox = _fox(r3(qa), r3(ka), r3(vf))

    kit = jnp.swapaxes(misc3[:, :, MISC_KI:MISC_KI + IDX_DIM], 1, 2).astype(BF16)
    sel_bias = _index(r3(qi), misc3, kit, topk)
    by_dist = rel_bias[_t5_bucket(jnp.arange(LANES, dtype=jnp.int32))] - rel_bias[N_BUCKETS - 1][None, :]
    a_dsa = _dsa((by_dist.T * LOG2E).astype(F32), r3(qd), r3(kd), r3(vd), sel_bias)

    x1, h2 = _merge(a_fox.reshape(m, FOX_W), a_dsa.reshape(m, DSA_W), gates, gates, x2, mod3,
                    wb(w_out_fox), wb(w_out_dsa), wb(w_out), norm2_g.reshape(1, d), seq)
    out = _ffn(h2, wb(w_ffn_in), conv_w, conv_b.reshape(1, -1), wb(w_ffn_out), x1, mod3, seq)
    return out.reshape(b, seq, d)


def kernel(x, c, w_ada, b_ada, norm1_g, w_in, b_forget, q_norm_fox, k_norm_fox, kv_norm_g, w_ukv, q_norm_dsa, k_norm_dsa, w_out_fox, w_out_dsa, w_out, norm2_g, w_ffn_in, conv_w, conv_b, w_ffn_out, rel_bias):
    b = x.shape[0]
    c8 = jnp.zeros((8, c.shape[1]), c.dtype).at[:b].set(c)
    for l in range(w_ada.shape[0]):
        x = _layer(x, c8, w_ada[l], b_ada[l], norm1_g[l], w_in[l], b_forget[l], q_norm_fox[l], k_norm_fox[l],
                   kv_norm_g[l], w_ukv[l], q_norm_dsa[l], k_norm_dsa[l], w_out_fox[l], w_out_dsa[l], w_out[l],
                   norm2_g[l], w_ffn_in[l], conv_w[l], conv_b[l], w_ffn_out[l], rel_bias)
    return x
```
